```python
import math
import jax, jax.numpy as jnp
from jax import lax
import numpy as np

D_MODEL = 1024
BATCH = 16
SEQ = 2048
DEPTH = 2

N_A_LAYERS = DEPTH // 2
N_B_LAYERS = DEPTH - N_A_LAYERS

RET_HEADS = 4
RET_QK_DIM = D_MODEL // RET_HEADS
RET_V_DIM = 2 * RET_QK_DIM
RET_CHUNK = 128
ROPE_BASE = 10000.0

SB_HEADS = 16
SB_HEAD_DIM = D_MODEL // SB_HEADS
SB_BLOCK = 128

D_FF = ((8 * D_MODEL // 3 + 127) // 128) * 128
CONV_WIDTH = 3
EPS = 1e-6

kernel_name = 'yoco_retention_stickbreaking_convffn_adaln'


def rms_norm(x, gain):
    x32 = x.astype(jnp.float32)
    y = x32 * lax.rsqrt(jnp.mean(x32 * x32, axis=-1, keepdims=True) + EPS)
    return (y * gain.astype(jnp.float32)).astype(x.dtype)


def modulate(xn, shift, scale):
    return (xn * (1.0 + scale[:, None, :]) + shift[:, None, :]).astype(xn.dtype)


def rotary(x, positions):
    half = x.shape[-1] // 2
    inv = ROPE_BASE ** (-jnp.arange(half, dtype=jnp.float32) / half)
    ang = positions.astype(jnp.float32)[..., None] * inv
    cos = jnp.cos(ang)[:, :, None, :]
    sin = jnp.sin(ang)[:, :, None, :]
    x32 = x.astype(jnp.float32)
    x1, x2 = x32[..., :half], x32[..., half:]
    return jnp.concatenate([x1 * cos - x2 * sin, x1 * sin + x2 * cos], axis=-1)


def retention(h, positions, w_in, w_out):
    B, S, _ = h.shape
    H, dk, dv, C = RET_HEADS, RET_QK_DIM, RET_V_DIM, RET_CHUNK
    proj = h @ w_in
    q, k, v, g = jnp.split(proj, [H * dk, 2 * H * dk, 2 * H * dk + H * dv], axis=-1)
    q = rotary(q.reshape(B, S, H, dk), positions)
    k = rotary(k.reshape(B, S, H, dk), positions) * (dk ** -0.5)
    v = v.reshape(B, S, H, dv).astype(jnp.float32)
    log_gamma = jnp.log(1.0 - 2.0 ** (-5.0 - jnp.arange(H, dtype=jnp.float32)))
    N = S // C

    def to_chunks(t):
        d = t.shape[-1]
        return t.reshape(B, N, C, H, d).transpose(1, 0, 3, 2, 4)

    idx = jnp.arange(C, dtype=jnp.float32)
    rel = idx[:, None] - idx[None, :]
    intra = jnp.where(rel[None] >= 0, jnp.exp(jnp.maximum(rel, 0.0)[None] * log_gamma[:, None, None]), 0.0)
    q_decay = jnp.exp((idx + 1.0)[None, :] * log_gamma[:, None])
    k_decay = jnp.exp((C - 1.0 - idx)[None, :] * log_gamma[:, None])
    chunk_decay = jnp.exp(C * log_gamma)

    def step(state, xs):
        qc, kc, vc = xs
        scores = jnp.einsum('bhnd,bhmd->bhnm', qc, kc) * intra[None]
        inner = jnp.einsum('bhnm,bhmv->bhnv', scores, vc)
        cross = jnp.einsum('bhnd,bhdv->bhnv', qc * q_decay[None, :, :, None], state)
        state = state * chunk_decay[None, :, None, None] + jnp.einsum(
            'bhmd,bhmv->bhdv', kc * k_decay[None, :, :, None], vc)
        return state, inner + cross

    state0 = jnp.zeros((B, H, dk, dv), jnp.float32)
    _, o = lax.scan(step, state0, (to_chunks(q), to_chunks(k), to_chunks(v)))
    o = o.transpose(1, 0, 3, 2, 4).reshape(B, S, H, dv)
    o = o * lax.rsqrt(jnp.mean(o * o, axis=-1, keepdims=True) + EPS)
    o = o.reshape(B, S, H * dv).astype(h.dtype)
    return (jax.nn.silu(g) * o) @ w_out


def shared_kv(h, c_act, kv_ada_w, kv_ada_b, kv_norm_g, w_kv, k_norm_g):
    B, S, _ = h.shape
    shift, scale = jnp.split(c_act @ kv_ada_w + kv_ada_b, 2, axis=-1)
    hn = modulate(rms_norm(h, kv_norm_g), shift, scale)
    k, v = jnp.split(hn @ w_kv, 2, axis=-1)
    k = rms_norm(k.reshape(B, S, SB_HEADS, SB_HEAD_DIM), k_norm_g)
    v = v.reshape(B, S, SB_HEADS, SB_HEAD_DIM)
    return k.transpose(0, 2, 1, 3), v.transpose(0, 2, 1, 3)


def stick_breaking(h, k, v, w_q, q_gain, w_out):
    B, S, _ = h.shape
    H, dh, BLK = SB_HEADS, SB_HEAD_DIM, SB_BLOCK
    NB = S // BLK
    q = rms_norm((h @ w_q).reshape(B, S, H, dh), q_gain).transpose(0, 2, 1, 3)
    qb = q.reshape(B, H, NB, BLK, dh).transpose(2, 0, 1, 3, 4)
    k32 = k.astype(jnp.float32)
    v32 = v.astype(jnp.float32)
    kpos = jnp.arange(S)
    scale = dh ** -0.5

    def block(args):
        qblk, i = args
        qpos = i * BLK + jnp.arange(BLK)
        mask = kpos[None, :] < qpos[:, None]
        z = jnp.einsum('bhqd,bhkd->bhqk', qblk.astype(jnp.float32), k32) * scale
        log_beta = jax.nn.log_sigmoid(z)
        log_1mb = jnp.where(mask, log_beta - z, 0.0)
        between = lax.cumsum(log_1mb, axis=3, reverse=True) - log_1mb
        A = jnp.where(mask, jnp.exp(log_beta + between), 0.0)
        return jnp.einsum('bhqk,bhkd->bhqd', A, v32)

    o = lax.map(block, (qb, jnp.arange(NB)))
    o = o.transpose(1, 0, 3, 2, 4).reshape(B, S, H * dh).astype(h.dtype)
    return o @ w_out


def conv_ffn(h, w_in, conv_w, conv_b, w_out):
    S = h.shape[1]
    u = h @ w_in
    up = jnp.pad(u, ((0, 0), (CONV_WIDTH - 1, 0), (0, 0)))
    y = conv_b[None, None, :] + up[:, 0:S] * conv_w[0]
    for tap in range(1, CONV_WIDTH):
        y = y + up[:, tap:tap + S] * conv_w[tap]
    val, gate = jnp.split(y, 2, axis=-1)
    return (val * jax.nn.silu(gate)) @ w_out


def setup_inputs(seed: int = 0) -> dict:
    key = jax.random.key(seed)
    ks = jax.random.split(key, 24)
    D, F = D_MODEL, D_FF
    f32 = jnp.float32

    def nrm(k, shape, fan_in, mult=1.0):
        return jax.random.normal(k, shape, f32) * (mult * fan_in ** -0.5)

    def gain(k, shape):
        return 1.0 + 0.02 * jax.random.normal(k, shape, f32)

    ret_in_cols = 2 * RET_HEADS * RET_QK_DIM + 2 * RET_HEADS * RET_V_DIM
    offsets = jax.random.randint(ks[2], (BATCH, 1), 0, 1024, dtype=jnp.int32)
    positions = (offsets + jnp.arange(SEQ, dtype=jnp.int32)[None, :]).astype(jnp.int32)
    return {
        'x': jax.random.normal(ks[0], (BATCH, SEQ, D), f32),
        'c': jax.random.normal(ks[1], (BATCH, D), f32),
        'positions': positions,
        'ada_w': nrm(ks[3], (DEPTH, D, 6 * D), D, 0.5),
        'ada_b': 0.02 * jax.random.normal(ks[4], (DEPTH, 6 * D), f32),
        'norm_mix_g': gain(ks[5], (DEPTH, D)),
        'norm_ffn_g': gain(ks[6], (DEPTH, D)),
        'ret_w_in': nrm(ks[7], (N_A_LAYERS, D, ret_in_cols), D),
        'ret_w_out': nrm(ks[8], (N_A_LAYERS, RET_HEADS * RET_V_DIM, D), RET_HEADS * RET_V_DIM),
        'kv_ada_w': nrm(ks[9], (D, 2 * D), D, 0.5),
        'kv_ada_b': 0.02 * jax.random.normal(ks[10], (2 * D,), f32),
        'kv_norm_g': gain(ks[11], (D,)),
        'w_kv': nrm(ks[12], (D, 2 * D), D),
        'k_norm_g': gain(ks[13], (SB_HEAD_DIM,)),
        'sb_w_q': nrm(ks[14], (N_B_LAYERS, D, D), D),
        'q_norm_g': gain(ks[15], (N_B_LAYERS, SB_HEAD_DIM)),
        'sb_w_out': nrm(ks[16], (N_B_LAYERS, D, D), D),
        'ffn_w_in': nrm(ks[17], (DEPTH, D, 2 * F), D),
        'ffn_conv_w': nrm(ks[18], (DEPTH, CONV_WIDTH, 2 * F), CONV_WIDTH),
        'ffn_conv_b': 0.02 * jax.random.normal(ks[19], (DEPTH, 2 * F), f32),
        'ffn_w_out': nrm(ks[20], (DEPTH, F, D), F),
    }


def reference(x, c, positions, ada_w, ada_b, norm_mix_g, norm_ffn_g, ret_w_in, ret_w_out,
              kv_ada_w, kv_ada_b, kv_norm_g, w_kv, k_norm_g, sb_w_q, q_norm_g, sb_w_out,
              ffn_w_in, ffn_conv_w, ffn_conv_b, ffn_w_out):
    c_act = jax.nn.silu(c)
    mods = jnp.einsum('bd,lde->lbe', c_act, ada_w) + ada_b[:, None, :]
    h = x
    k_sh = None
    v_sh = None
    for layer in range(DEPTH):
        shift_m, scale_m, gate_m, shift_f, scale_f, gate_f = jnp.split(mods[layer], 6, axis=-1)
        hn = modulate(rms_norm(h, norm_mix_g[layer]), shift_m, scale_m)
        if layer < N_A_LAYERS:
            mix = retention(hn, positions, ret_w_in[layer], ret_w_out[layer])
        else:
            j = layer - N_A_LAYERS
            if j == 0:
                k_sh, v_sh = shared_kv(h, c_act, kv_ada_w, kv_ada_b, kv_norm_g, w_kv, k_norm_g)
            mix = stick_breaking(hn, k_sh, v_sh, sb_w_q[j], q_norm_g[j], sb_w_out[j])
        h = h + (gate_m[:, None, :] * mix).astype(h.dtype)
        hf = modulate(rms_norm(h, norm_ffn_g[layer]), shift_f, scale_f)
        ff = conv_ffn(hf, ffn_w_in[layer], ffn_conv_w[layer], ffn_conv_b[layer], ffn_w_out[layer])
        h = h + (gate_f[:, None, :] * ff).astype(h.dtype)
    return h
```

```python
import functools
import math

import jax
import jax.numpy as jnp
from jax import lax
from jax.experimental import pallas as pl
from jax.experimental.pallas import tpu as pltpu

F32 = jnp.float32
BF16 = jnp.bfloat16

EPS = 1e-6
ROPE_BASE = 10000.0
RET_HEADS = 4
RET_CHUNK = 128
SB_HEADS = 16
CONV_WIDTH = 3

LANES = 128
SUBLANES = 8
MXU_DIM = 256
V7X_VMEM_BYTES = 64 * 1024 * 1024
VMEM_LIMIT_BYTES = (V7X_VMEM_BYTES * 7) // 8

ROW_TILE = 256
OUT_ROW_TILE = 512
ATTN_TILE = 256


def _params(*semantics):
    return pltpu.CompilerParams(dimension_semantics=semantics,
                                vmem_limit_bytes=VMEM_LIMIT_BYTES)


def _resident(shape):
    zeros = (0,) * len(shape)
    return pl.BlockSpec(shape, lambda *_: zeros, pipeline_mode=pl.Buffered(1))


def _silu(x):
    return x / (1.0 + jnp.exp(-x))


def _unit_rms(x):
    return x * lax.rsqrt(jnp.mean(x * x, axis=-1, keepdims=True) + EPS)


def _mxu(a, b):
    return jnp.dot(a.astype(BF16), b.astype(BF16), preferred_element_type=F32)


def _mxu_nt(a, b):
    return lax.dot_general(a.astype(BF16), b.astype(BF16), (((1,), (1,)), ((), ())),
                           preferred_element_type=F32)


def _mxu_tn(a, b):
    return lax.dot_general(a.astype(BF16), b.astype(BF16), (((0,), (0,)), ((), ())),
                           preferred_element_type=F32)


def _ada_kernel(c_ref, w_ref, b_ref, o_ref):
    o_ref[...] = _mxu(_silu(c_ref[...]), w_ref[...]) + b_ref[...]


def _ada(c, w, b, col_tile=2048):
    nb, d = c.shape
    nl, _, n = w.shape
    return pl.pallas_call(
        _ada_kernel,
        grid=(nl, n // col_tile),
        in_specs=[
            pl.BlockSpec((nb, d), lambda l, j: (0, 0)),
            pl.BlockSpec((None, d, col_tile), lambda l, j: (l, 0, j)),
            pl.BlockSpec((None, 1, col_tile), lambda l, j: (l, 0, j)),
        ],
        out_specs=pl.BlockSpec((None, nb, col_tile), lambda l, j: (l, 0, j)),
        out_shape=jax.ShapeDtypeStruct((nl, nb, n), F32),
        compiler_params=_params("parallel", "parallel"),
        name="ada_mods",
    )(c, w, b.reshape(nl, 1, n))


def _rope_kernel(pos_ref, cos_ref, sin_ref, *, half):
    i = lax.broadcasted_iota(jnp.int32, (1, half), 1).astype(F32)
    inv = jnp.exp(i * (-math.log(ROPE_BASE) / half))
    ang = pos_ref[0].astype(F32) * inv
    cos_ref[0] = jnp.cos(ang)
    sin_ref[0] = jnp.sin(ang)


def _rope_tables(positions, half):
    nb, s = positions.shape
    out = jax.ShapeDtypeStruct((nb, s, half), F32)
    return pl.pallas_call(
        functools.partial(_rope_kernel, half=half),
        grid=(nb,),
        in_specs=[pl.BlockSpec((1, s, 1), lambda b: (b, 0, 0))],
        out_specs=[pl.BlockSpec((1, s, half), lambda b: (b, 0, 0))] * 2,
        out_shape=[out, out],
        compiler_params=_params("parallel"),
        name="rope_tables",
    )(positions.reshape(nb, s, 1))


def _ret_proj_kernel(h_ref, mod_ref, g_ref, w_ref, cos_ref, sin_ref, o_ref, *, qk_cols, k_scale):
    mod = mod_ref[0]
    hn = (_unit_rms(h_ref[...]) * g_ref[...] * (1.0 + mod[1:2, :]) + mod[0:1, :]).astype(BF16)
    cos = cos_ref[...]
    sin = sin_ref[...]
    half = cos.shape[-1]
    n = w_ref.shape[1]
    for c in range(0, 2 * qk_cols, 2 * half):
        acc = jnp.dot(hn, w_ref[:, c:c + 2 * half], preferred_element_type=F32)
        x1 = acc[:, :half]
        x2 = acc[:, half:]
        sc = k_scale if c >= qk_cols else 1.0
        o_ref[:, c:c + half] = ((x1 * cos - x2 * sin) * sc).astype(BF16)
        o_ref[:, c + half:c + 2 * half] = ((x1 * sin + x2 * cos) * sc).astype(BF16)
    step = 2 * MXU_DIM
    for c in range(2 * qk_cols, n, step):
        o_ref[:, c:c + step] = jnp.dot(hn, w_ref[:, c:c + step],
                                       preferred_element_type=F32).astype(BF16)


def _ret_proj(h, mods, gain, w, cos, sin, seq, qk_cols, k_scale):
    t, d = h.shape
    n = w.shape[1]
    tm = ROW_TILE
    per_seq = seq // tm
    half = cos.shape[-1]
    return pl.pallas_call(
        functools.partial(_ret_proj_kernel, qk_cols=qk_cols, k_scale=k_scale),
        grid=(t // tm,),
        in_specs=[
            pl.BlockSpec((tm, d), lambda i: (i, 0)),
            pl.BlockSpec((1,) + mods.shape[1:], lambda i: (i // per_seq, 0, 0)),
            _resident((1, d)),
            _resident((d, n)),
            pl.BlockSpec((tm, half), lambda i: (i, 0)),
            pl.BlockSpec((tm, half), lambda i: (i, 0)),
        ],
        out_specs=pl.BlockSpec((tm, n), lambda i: (i, 0)),
        out_shape=jax.ShapeDtypeStruct((t, n), BF16),
        compiler_params=_params("parallel"),
        name="ret_proj",
    )(h, mods, gain.reshape(1, d), w, cos, sin)


def _retention_kernel(q_ref, k_ref, v_ref, g_ref, o_ref, state_ref, *, chunk, n_chunks):
    head = jnp.full((1, 1), pl.program_id(1), jnp.int32).astype(F32)
    log_gamma = jnp.log(1.0 - jnp.exp2(-5.0 - head))
    row = lax.broadcasted_iota(jnp.int32, (chunk, chunk), 0).astype(F32)
    col = lax.broadcasted_iota(jnp.int32, (chunk, chunk), 1).astype(F32)
    rel = row - col
    intra = jnp.where(rel >= 0, jnp.exp(jnp.maximum(rel, 0.0) * log_gamma), 0.0)
    idx = row[:, 0:1]
    q_decay = jnp.exp((idx + 1.0) * log_gamma)
    k_decay = jnp.exp((chunk - 1.0 - idx) * log_gamma)
    chunk_decay = jnp.exp(chunk * log_gamma)

    state_ref[...] = jnp.zeros_like(state_ref)

    def step(n, carry):
        r0 = pl.multiple_of(n * chunk, chunk)
        qc = q_ref[0, pl.ds(r0, chunk), :]
        kc = k_ref[0, pl.ds(r0, chunk), :]
        vc = v_ref[0, pl.ds(r0, chunk), :]
        state = state_ref[...]
        scores = _mxu_nt(qc, kc) * intra
        inner = _mxu(scores, vc)
        cross = _mxu(qc.astype(F32) * q_decay, state)
        state_ref[...] = state * chunk_decay + _mxu_tn(kc.astype(F32) * k_decay, vc)
        o = _unit_rms(inner + cross)
        gate = _silu(g_ref[0, pl.ds(r0, chunk), :].astype(F32))
        o_ref[0, pl.ds(r0, chunk), :] = (gate * o).astype(BF16)
        return carry

    lax.fori_loop(0, n_chunks, step, 0)


def _retention(proj, heads, dk, dv, chunk):
    nb, s, _ = proj.shape
    kb0 = heads * dk // dk
    vb0 = 2 * heads * dk // dv
    gb0 = vb0 + heads
    return pl.pallas_call(
        functools.partial(_retention_kernel, chunk=chunk, n_chunks=s // chunk),
        grid=(nb, heads),
        in_specs=[
            pl.BlockSpec((1, s, dk), lambda b, h: (b, 0, h)),
            pl.BlockSpec((1, s, dk), lambda b, h: (b, 0, kb0 + h)),
            pl.BlockSpec((1, s, dv), lambda b, h: (b, 0, vb0 + h)),
            pl.BlockSpec((1, s, dv), lambda b, h: (b, 0, gb0 + h)),
        ],
        out_specs=pl.BlockSpec((1, s, dv), lambda b, h: (b, 0, h)),
        out_shape=jax.ShapeDtypeStruct((nb, s, heads * dv), BF16),
        scratch_shapes=[pltpu.VMEM((dk, dv), F32)],
        compiler_params=_params("parallel", "parallel"),
        name="retention",
    )(proj, proj, proj, proj)


def _out_proj_kernel(a_ref, w_ref, h_ref, mod_ref, o_ref, *, gate_row):
    gate = mod_ref[0][gate_row:gate_row + 1, :]
    o_ref[...] = h_ref[...] + gate * jnp.dot(a_ref[...], w_ref[...], preferred_element_type=F32)


def _out_proj(a, w, h, mods, gate_row, seq):
    t, k = a.shape
    d = w.shape[1]
    tm = OUT_ROW_TILE
    per_seq = seq // tm
    return pl.pallas_call(
        functools.partial(_out_proj_kernel, gate_row=gate_row),
        grid=(t // tm,),
        in_specs=[
            pl.BlockSpec((tm, k), lambda i: (i, 0)),
            _resident((k, d)),
            pl.BlockSpec((tm, d), lambda i: (i, 0)),
            pl.BlockSpec((1,) + mods.shape[1:], lambda i: (i // per_seq, 0, 0)),
        ],
        out_specs=pl.BlockSpec((tm, d), lambda i: (i, 0)),
        out_shape=jax.ShapeDtypeStruct((t, d), F32),
        compiler_params=_params("parallel"),
        name="out_proj",
    )(a, w, h, mods)


def _ffn_up_kernel(h_ref, mod_ref, g_ref, w_ref, cw_ref, cb_ref, o_ref, carry_ref, buf_ref,
                   *, per_seq, col_tile):
    tm = h_ref.shape[0]
    f = o_ref.shape[1]
    halo = SUBLANES

    @pl.when(pl.program_id(0) % per_seq == 0)
    def _():
        carry_ref[...] = jnp.zeros_like(carry_ref)

    mod = mod_ref[0]
    hn = (_unit_rms(h_ref[...]) * g_ref[...] * (1.0 + mod[4:5, :]) + mod[3:4, :]).astype(BF16)

    def conv(c):
        u = jnp.dot(hn, w_ref[:, c:c + col_tile], preferred_element_type=F32)
        buf_ref[0:halo, :] = carry_ref[:, c:c + col_tile]
        buf_ref[halo:halo + tm, :] = u
        carry_ref[:, c:c + col_tile] = u[tm - halo:, :]
        y = cb_ref[:, c:c + col_tile] + buf_ref[halo - 2:halo - 2 + tm, :] * cw_ref[0:1, c:c + col_tile]
        y = y + buf_ref[halo - 1:halo - 1 + tm, :] * cw_ref[1:2, c:c + col_tile]
        return y + u * cw_ref[2:3, c:c + col_tile]

    for c in range(0, f, col_tile):
        val = conv(c)
        gate = conv(f + c)
        o_ref[:, c:c + col_tile] = (val * _silu(gate)).astype(BF16)


def _ffn_up(h, mods, gain, w, conv_w, conv_b, seq):
    t, d = h.shape
    f2 = w.shape[1]
    f = f2 // 2
    tm = ROW_TILE
    per_seq = seq // tm
    col_tile = MXU_DIM
    return pl.pallas_call(
        functools.partial(_ffn_up_kernel, per_seq=per_seq, col_tile=col_tile),
        grid=(t // tm,),
        in_specs=[
            pl.BlockSpec((tm, d), lambda i: (i, 0)),
            pl.BlockSpec((1,) + mods.shape[1:], lambda i: (i // per_seq, 0, 0)),
            _resident((1, d)),
            _resident((d, f2)),
            _resident((CONV_WIDTH, f2)),
            _resident((1, f2)),
        ],
        out_specs=pl.BlockSpec((tm, f), lambda i: (i, 0)),
        out_shape=jax.ShapeDtypeStruct((t, f), BF16),
        scratch_shapes=[pltpu.VMEM((SUBLANES, f2), F32),
                        pltpu.VMEM((SUBLANES + tm, col_tile), F32)],
        compiler_params=_params("arbitrary"),
        name="ffn_up",
    )(h, mods, gain.reshape(1, d), w, conv_w, conv_b.reshape(1, f2))


def _head_rms(x, pool):
    ms = jnp.dot((x * x).astype(BF16), pool, preferred_element_type=F32)
    return x * lax.rsqrt(ms + EPS)


def _qkv_proj_kernel(h_ref, mod_ref, kvmod_ref, gm_ref, gkv_ref, wq_ref, wkv_ref, qg_ref, kg_ref,
                     q_ref, k_ref, v_ref, *, head_dim, q_scale):
    d = h_ref.shape[1]
    xn = _unit_rms(h_ref[...])
    mod = mod_ref[0]
    kvmod = kvmod_ref[0]
    hq = (xn * gm_ref[...] * (1.0 + mod[1:2, :]) + mod[0:1, :]).astype(BF16)
    hkv = (xn * gkv_ref[...] * (1.0 + kvmod[1:2, :]) + kvmod[0:1, :]).astype(BF16)
    ct = MXU_DIM
    r = lax.broadcasted_iota(jnp.int32, (ct, ct), 0) // head_dim
    c = lax.broadcasted_iota(jnp.int32, (ct, ct), 1) // head_dim
    pool = jnp.where(r == c, 1.0 / head_dim, 0.0).astype(BF16)
    for c0 in range(0, d, ct):
        q = jnp.dot(hq, wq_ref[:, c0:c0 + ct], preferred_element_type=F32)
        q_ref[:, c0:c0 + ct] = (_head_rms(q, pool) * (qg_ref[:, c0:c0 + ct] * q_scale)).astype(BF16)
        k = jnp.dot(hkv, wkv_ref[:, c0:c0 + ct], preferred_element_type=F32)
        k_ref[:, c0:c0 + ct] = (_head_rms(k, pool) * kg_ref[:, c0:c0 + ct]).astype(BF16)
        v_ref[:, c0:c0 + ct] = jnp.dot(hkv, wkv_ref[:, d + c0:d + c0 + ct],
                                       preferred_element_type=F32).astype(BF16)


def _qkv_proj(h, mods, kvmods, g_mix, g_kv, wq, wkv, q_gain, k_gain, seq, head_dim):
    t, d = h.shape
    tm = ROW_TILE
    per_seq = seq // tm
    out = jax.ShapeDtypeStruct((t, d), BF16)
    row_spec = pl.BlockSpec((tm, d), lambda i: (i, 0))
    return pl.pallas_call(
        functools.partial(_qkv_proj_kernel, head_dim=head_dim, q_scale=head_dim ** -0.5),
        grid=(t // tm,),
        in_specs=[
            row_spec,
            pl.BlockSpec((1,) + mods.shape[1:], lambda i: (i // per_seq, 0, 0)),
            pl.BlockSpec((1,) + kvmods.shape[1:], lambda i: (i // per_seq, 0, 0)),
            _resident((1, d)),
            _resident((1, d)),
            _resident((d, d)),
            _resident((d, 2 * d)),
            _resident((1, d)),
            _resident((1, d)),
        ],
        out_specs=[row_spec, row_spec, row_spec],
        out_shape=[out, out, out],
        compiler_params=_params("parallel"),
        name="qkv_proj",
    )(h, mods, kvmods, g_mix.reshape(1, d), g_kv.reshape(1, d), wq, wkv,
      jnp.tile(q_gain, d // head_dim).reshape(1, d), jnp.tile(k_gain, d // head_dim).reshape(1, d))


def _neg_softplus(z):
    return -(jnp.maximum(z, 0.0) + jnp.log(1.0 + jnp.exp(-jnp.abs(z))))


def _suffix_sum(x, upper):
    hi = x.astype(BF16)
    lo = (x - hi.astype(F32)).astype(BF16)
    return (jnp.dot(hi, upper, preferred_element_type=F32)
            + jnp.dot(lo, upper, preferred_element_type=F32))


def _attn_kernel(q_ref, k_ref, v_ref, o_ref, *, tile, head_dim):
    n_blocks = q_ref.shape[1] // tile
    lane = lax.broadcasted_iota(jnp.int32, (1, LANES), 1)
    row = lax.broadcasted_iota(jnp.int32, (tile, tile), 0)
    col = lax.broadcasted_iota(jnp.int32, (tile, tile), 1)
    upper = (row > col).astype(BF16)
    causal = col < row

    def q_block(qi, carry):
        q0 = pl.multiple_of(qi * tile, tile)
        qb = q_ref[0, pl.ds(q0, tile), :]
        outs = []
        for hd in range(LANES // head_dim):
            in_head = (lane >= hd * head_dim) & (lane < (hd + 1) * head_dim)
            qh = jnp.where(in_head, qb, jnp.zeros_like(qb))

            z = _mxu_nt(qh, k_ref[0, pl.ds(q0, tile), :])
            lg = jnp.where(causal, _neg_softplus(z), 0.0)
            after = _suffix_sum(lg, upper)
            a = jnp.where(causal, jnp.exp(z + lg + after), 0.0)
            acc = _mxu(a, v_ref[0, pl.ds(q0, tile), :])
            later = after[:, 0:1] + lg[:, 0:1]

            def k_block(j, st):
                acc, later = st
                k0 = pl.multiple_of((qi - 1 - j) * tile, tile)
                z = _mxu_nt(qh, k_ref[0, pl.ds(k0, tile), :])
                lg = _neg_softplus(z)
                after = _suffix_sum(lg, upper)
                a = jnp.exp(z + lg + after + later)
                acc = acc + _mxu(a, v_ref[0, pl.ds(k0, tile), :])
                return acc, later + after[:, 0:1] + lg[:, 0:1]

            acc, _ = lax.fori_loop(0, qi, k_block, (acc, later))
            outs.append(jnp.where(in_head, acc, 0.0))
        o_ref[0, pl.ds(q0, tile), :] = sum(outs).astype(BF16)
        return carry

    lax.fori_loop(0, n_blocks, q_block, 0)


def _attention(q, k, v, head_dim):
    nb, s, d = q.shape
    spec = pl.BlockSpec((1, s, LANES), lambda b, p: (b, 0, p))
    return pl.pallas_call(
        functools.partial(_attn_kernel, tile=ATTN_TILE, head_dim=head_dim),
        grid=(nb, d // LANES),
        in_specs=[spec, spec, spec],
        out_specs=spec,
        out_shape=jax.ShapeDtypeStruct((nb, s, d), BF16),
        compiler_params=_params("parallel", "parallel"),
        name="sb_attention",
    )(q, k, v)


def kernel(x, c, positions, ada_w, ada_b, norm_mix_g, norm_ffn_g, ret_w_in, ret_w_out, kv_ada_w, kv_ada_b, kv_norm_g, w_kv, k_norm_g, sb_w_q, q_norm_g, sb_w_out, ffn_w_in, ffn_conv_w, ffn_conv_b, ffn_w_out):
    nb, s, d = x.shape
    t = nb * s
    depth = ada_w.shape[0]
    n_mod = ada_w.shape[2] // d
    dk = d // RET_HEADS
    dv = 2 * dk
    head_dim = d // SB_HEADS

    mods = _ada(c, ada_w, ada_b).reshape(depth, nb, n_mod, d)
    kvmods = _ada(c, kv_ada_w[None], kv_ada_b[None]).reshape(nb, 2, d)
    cos, sin = _rope_tables(positions, dk // 2)
    cos = cos.reshape(t, dk // 2)
    sin = sin.reshape(t, dk // 2)

    h = x.reshape(t, d)

    proj = _ret_proj(h, mods[0], norm_mix_g[0], ret_w_in[0].astype(BF16), cos, sin,
                     s, RET_HEADS * dk, dk ** -0.5)
    og = _retention(proj.reshape(nb, s, -1), RET_HEADS, dk, dv, RET_CHUNK)
    h = _out_proj(og.reshape(t, -1), ret_w_out[0].astype(BF16), h, mods[0], 2, s)
    act = _ffn_up(h, mods[0], norm_ffn_g[0], ffn_w_in[0].astype(BF16), ffn_conv_w[0], ffn_conv_b[0], s)
    h = _out_proj(act, ffn_w_out[0].astype(BF16), h, mods[0], 5, s)

    q, k, v = _qkv_proj(h, mods[1], kvmods, norm_mix_g[1], kv_norm_g, sb_w_q[0].astype(BF16),
                        w_kv.astype(BF16), q_norm_g[0], k_norm_g, s, head_dim)
    o = _attention(q.reshape(nb, s, d), k.reshape(nb, s, d), v.reshape(nb, s, d), head_dim)
    h = _out_proj(o.reshape(t, d), sb_w_out[0].astype(BF16), h, mods[1], 2, s)
    act = _ffn_up(h, mods[1], norm_ffn_g[1], ffn_w_in[1].astype(BF16), ffn_conv_w[1], ffn_conv_b[1], s)
    h = _out_proj(act, ffn_w_out[1].astype(BF16), h, mods[1], 5, s)
    return h.reshape(nb, s, d)
```

```python
import functools
import math

import jax
import jax.numpy as jnp
from jax import lax
from jax.experimental import pallas as pl
from jax.experimental.pallas import tpu as pltpu

F32 = jnp.float32
BF16 = jnp.bfloat16

EPS = 1e-6
ROPE_BASE = 10000.0
RET_HEADS = 4
RET_CHUNK = 128
SB_HEADS = 16
CONV_WIDTH = 3

LANES = 128
SUBLANES = 8
MXU_DIM = 256
V7X_VMEM_BYTES = 64 * 1024 * 1024
VMEM_LIMIT_BYTES = (V7X_VMEM_BYTES * 7) // 8

ROW_TILE = 256
OUT_ROW_TILE = 512
ATTN_TILE = 256


def _params(*semantics):
    return pltpu.CompilerParams(dimension_semantics=semantics,
                                vmem_limit_bytes=VMEM_LIMIT_BYTES)


def _resident(shape):
    zeros = (0,) * len(shape)
    return pl.BlockSpec(shape, lambda *_: zeros, pipeline_mode=pl.Buffered(1))


def _silu(x):
    return x / (1.0 + jnp.exp(-x))


def _unit_rms(x):
    return x * lax.rsqrt(jnp.mean(x * x, axis=-1, keepdims=True) + EPS)


def _mxu(a, b):
    return jnp.dot(a.astype(BF16), b.astype(BF16), preferred_element_type=F32)


def _mxu_nt(a, b):
    return lax.dot_general(a.astype(BF16), b.astype(BF16), (((1,), (1,)), ((), ())),
                           preferred_element_type=F32)


def _mxu_tn(a, b):
    return lax.dot_general(a.astype(BF16), b.astype(BF16), (((0,), (0,)), ((), ())),
                           preferred_element_type=F32)


def _ada_kernel(c_ref, w_ref, b_ref, o_ref):
    o_ref[...] = _mxu(_silu(c_ref[...]), w_ref[...]) + b_ref[...]


def _ada(c, w, b, col_tile=2048):
    nb, d = c.shape
    nl, _, n = w.shape
    return pl.pallas_call(
        _ada_kernel,
        grid=(nl, n // col_tile),
        in_specs=[
            pl.BlockSpec((nb, d), lambda l, j: (0, 0)),
            pl.BlockSpec((None, d, col_tile), lambda l, j: (l, 0, j)),
            pl.BlockSpec((None, 1, col_tile), lambda l, j: (l, 0, j)),
        ],
        out_specs=pl.BlockSpec((None, nb, col_tile), lambda l, j: (l, 0, j)),
        out_shape=jax.ShapeDtypeStruct((nl, nb, n), F32),
        compiler_params=_params("parallel", "parallel"),
        name="ada_mods",
    )(c, w, b.reshape(nl, 1, n))


def _rope_kernel(pos_ref, cos_ref, sin_ref, *, half):
    i = lax.broadcasted_iota(jnp.int32, (1, half), 1).astype(F32)
    inv = jnp.exp(i * (-math.log(ROPE_BASE) / half))
    ang = pos_ref[0].astype(F32) * inv
    cos_ref[0] = jnp.cos(ang)
    sin_ref[0] = jnp.sin(ang)


def _rope_tables(positions, half):
    nb, s = positions.shape
    out = jax.ShapeDtypeStruct((nb, s, half), F32)
    return pl.pallas_call(
        functools.partial(_rope_kernel, half=half),
        grid=(nb,),
        in_specs=[pl.BlockSpec((1, s, 1), lambda b: (b, 0, 0))],
        out_specs=[pl.BlockSpec((1, s, half), lambda b: (b, 0, 0))] * 2,
        out_shape=[out, out],
        compiler_params=_params("parallel"),
        name="rope_tables",
    )(positions.reshape(nb, s, 1))


def _ret_proj_kernel(h_ref, mod_ref, g_ref, w_ref, cos_ref, sin_ref, o_ref, *, qk_cols, k_scale):
    mod = mod_ref[0]
    hn = (_unit_rms(h_ref[...]) * g_ref[...] * (1.0 + mod[1:2, :]) + mod[0:1, :]).astype(BF16)
    cos = cos_ref[...]
    sin = sin_ref[...]
    half = cos.shape[-1]
    n = w_ref.shape[1]
    for c in range(0, 2 * qk_cols, 2 * half):
        acc = jnp.dot(hn, w_ref[:, c:c + 2 * half], preferred_element_type=F32)
        x1 = acc[:, :half]
        x2 = acc[:, half:]
        sc = k_scale if c >= qk_cols else 1.0
        o_ref[:, c:c + half] = ((x1 * cos - x2 * sin) * sc).astype(BF16)
        o_ref[:, c + half:c + 2 * half] = ((x1 * sin + x2 * cos) * sc).astype(BF16)
    step = 2 * MXU_DIM
    for c in range(2 * qk_cols, n, step):
        o_ref[:, c:c + step] = jnp.dot(hn, w_ref[:, c:c + step],
                                       preferred_element_type=F32).astype(BF16)


def _ret_proj(h, mods, gain, w, cos, sin, seq, qk_cols, k_scale):
    t, d = h.shape
    n = w.shape[1]
    tm = ROW_TILE
    per_seq = seq // tm
    half = cos.shape[-1]
    return pl.pallas_call(
        functools.partial(_ret_proj_kernel, qk_cols=qk_cols, k_scale=k_scale),
        grid=(t // tm,),
        in_specs=[
            pl.BlockSpec((tm, d), lambda i: (i, 0)),
            pl.BlockSpec((1,) + mods.shape[1:], lambda i: (i // per_seq, 0, 0)),
            _resident((1, d)),
            _resident((d, n)),
            pl.BlockSpec((tm, half), lambda i: (i, 0)),
            pl.BlockSpec((tm, half), lambda i: (i, 0)),
        ],
        out_specs=pl.BlockSpec((tm, n), lambda i: (i, 0)),
        out_shape=jax.ShapeDtypeStruct((t, n), BF16),
        compiler_params=_params("parallel"),
        name="ret_proj",
    )(h, mods, gain.reshape(1, d), w, cos, sin)


def _retention_kernel(q_ref, k_ref, v_ref, g_ref, o_ref, state_ref, *, chunk, n_chunks):
    head = jnp.full((1, 1), pl.program_id(1), jnp.int32).astype(F32)
    log_gamma = jnp.log(1.0 - jnp.exp2(-5.0 - head))
    row = lax.broadcasted_iota(jnp.int32, (chunk, chunk), 0).astype(F32)
    col = lax.broadcasted_iota(jnp.int32, (chunk, chunk), 1).astype(F32)
    rel = row - col
    intra = jnp.where(rel >= 0, jnp.exp(jnp.maximum(rel, 0.0) * log_gamma), 0.0)
    idx = row[:, 0:1]
    q_decay = jnp.exp((idx + 1.0) * log_gamma)
    k_decay = jnp.exp((chunk - 1.0 - idx) * log_gamma)
    chunk_decay = jnp.exp(chunk * log_gamma)

    state_ref[...] = jnp.zeros_like(state_ref)

    def step(n, carry):
        r0 = pl.multiple_of(n * chunk, chunk)
        qc = q_ref[0, pl.ds(r0, chunk), :]
        kc = k_ref[0, pl.ds(r0, chunk), :]
        vc = v_ref[0, pl.ds(r0, chunk), :]
        state = state_ref[...]
        scores = _mxu_nt(qc, kc) * intra
        inner = _mxu(scores, vc)
        cross = _mxu(qc.astype(F32) * q_decay, state)
        state_ref[...] = state * chunk_decay + _mxu_tn(kc.astype(F32) * k_decay, vc)
        o = _unit_rms(inner + cross)
        gate = _silu(g_ref[0, pl.ds(r0, chunk), :].astype(F32))
        o_ref[0, pl.ds(r0, chunk), :] = (gate * o).astype(BF16)
        return carry

    lax.fori_loop(0, n_chunks, step, 0)


def _retention(proj, heads, dk, dv, chunk):
    nb, s, _ = proj.shape
    kb0 = heads * dk // dk
    vb0 = 2 * heads * dk // dv
    gb0 = vb0 + heads
    return pl.pallas_call(
        functools.partial(_retention_kernel, chunk=chunk, n_chunks=s // chunk),
        grid=(nb, heads),
        in_specs=[
            pl.BlockSpec((1, s, dk), lambda b, h: (b, 0, h)),
            pl.BlockSpec((1, s, dk), lambda b, h: (b, 0, kb0 + h)),
            pl.BlockSpec((1, s, dv), lambda b, h: (b, 0, vb0 + h)),
            pl.BlockSpec((1, s, dv), lambda b, h: (b, 0, gb0 + h)),
        ],
        out_specs=pl.BlockSpec((1, s, dv), lambda b, h: (b, 0, h)),
        out_shape=jax.ShapeDtypeStruct((nb, s, heads * dv), BF16),
        scratch_shapes=[pltpu.VMEM((dk, dv), F32)],
        compiler_params=_params("parallel", "parallel"),
        name="retention",
    )(proj, proj, proj, proj)


def _out_proj_kernel(a_ref, w_ref, h_ref, mod_ref, o_ref, *, gate_row):
    gate = mod_ref[0][gate_row:gate_row + 1, :]
    o_ref[...] = h_ref[...] + gate * jnp.dot(a_ref[...], w_ref[...], preferred_element_type=F32)


def _out_proj(a, w, h, mods, gate_row, seq):
    t, k = a.shape
    d = w.shape[1]
    tm = OUT_ROW_TILE
    per_seq = seq // tm
    return pl.pallas_call(
        functools.partial(_out_proj_kernel, gate_row=gate_row),
        grid=(t // tm,),
        in_specs=[
            pl.BlockSpec((tm, k), lambda i: (i, 0)),
            _resident((k, d)),
            pl.BlockSpec((tm, d), lambda i: (i, 0)),
            pl.BlockSpec((1,) + mods.shape[1:], lambda i: (i // per_seq, 0, 0)),
        ],
        out_specs=pl.BlockSpec((tm, d), lambda i: (i, 0)),
        out_shape=jax.ShapeDtypeStruct((t, d), F32),
        compiler_params=_params("parallel"),
        name="out_proj",
    )(a, w, h, mods)


def _ffn_up_kernel(h_ref, mod_ref, g_ref, w_ref, cw_ref, cb_ref, o_ref, carry_ref, buf_ref,
                   *, per_seq, col_tile):
    tm = h_ref.shape[0]
    f = o_ref.shape[1]
    halo = SUBLANES

    @pl.when(pl.program_id(0) % per_seq == 0)
    def _():
        carry_ref[...] = jnp.zeros_like(carry_ref)

    mod = mod_ref[0]
    hn = (_unit_rms(h_ref[...]) * g_ref[...] * (1.0 + mod[4:5, :]) + mod[3:4, :]).astype(BF16)

    def conv(c):
        u = jnp.dot(hn, w_ref[:, c:c + col_tile], preferred_element_type=F32)
        buf_ref[0:halo, :] = carry_ref[:, c:c + col_tile]
        buf_ref[halo:halo + tm, :] = u
        carry_ref[:, c:c + col_tile] = u[tm - halo:, :]
        y = cb_ref[:, c:c + col_tile] + buf_ref[halo - 2:halo - 2 + tm, :] * cw_ref[0:1, c:c + col_tile]
        y = y + buf_ref[halo - 1:halo - 1 + tm, :] * cw_ref[1:2, c:c + col_tile]
        return y + u * cw_ref[2:3, c:c + col_tile]

    for c in range(0, f, col_tile):
        val = conv(c)
        gate = conv(f + c)
        o_ref[:, c:c + col_tile] = (val * _silu(gate)).astype(BF16)


def _ffn_up(h, mods, gain, w, conv_w, conv_b, seq):
    t, d = h.shape
    f2 = w.shape[1]
    f = f2 // 2
    tm = ROW_TILE
    per_seq = seq // tm
    col_tile = MXU_DIM
    return pl.pallas_call(
        functools.partial(_ffn_up_kernel, per_seq=per_seq, col_tile=col_tile),
        grid=(t // tm,),
        in_specs=[
            pl.BlockSpec((tm, d), lambda i: (i, 0)),
            pl.BlockSpec((1,) + mods.shape[1:], lambda i: (i // per_seq, 0, 0)),
            _resident((1, d)),
            _resident((d, f2)),
            _resident((CONV_WIDTH, f2)),
            _resident((1, f2)),
        ],
        out_specs=pl.BlockSpec((tm, f), lambda i: (i, 0)),
        out_shape=jax.ShapeDtypeStruct((t, f), BF16),
        scratch_shapes=[pltpu.VMEM((SUBLANES, f2), F32),
                        pltpu.VMEM((SUBLANES + tm, col_tile), F32)],
        compiler_params=_params("arbitrary"),
        name="ffn_up",
    )(h, mods, gain.reshape(1, d), w, conv_w, conv_b.reshape(1, f2))


def _head_rms(x, pool):
    ms = jnp.dot((x * x).astype(BF16), pool, preferred_element_type=F32)
    return x * lax.rsqrt(ms + EPS)


def _qkv_proj_kernel(h_ref, mod_ref, kvmod_ref, gm_ref, gkv_ref, wq_ref, wkv_ref, qg_ref, kg_ref,
                     q_ref, k_ref, v_ref, *, head_dim, q_scale):
    d = h_ref.shape[1]
    xn = _unit_rms(h_ref[...])
    mod = mod_ref[0]
    kvmod = kvmod_ref[0]
    hq = (xn * gm_ref[...] * (1.0 + mod[1:2, :]) + mod[0:1, :]).astype(BF16)
    hkv = (xn * gkv_ref[...] * (1.0 + kvmod[1:2, :]) + kvmod[0:1, :]).astype(BF16)
    ct = MXU_DIM
    r = lax.broadcasted_iota(jnp.int32, (ct, ct), 0) // head_dim
    c = lax.broadcasted_iota(jnp.int32, (ct, ct), 1) // head_dim
    pool = jnp.where(r == c, 1.0 / head_dim, 0.0).astype(BF16)
    for c0 in range(0, d, ct):
        q = jnp.dot(hq, wq_ref[:, c0:c0 + ct], preferred_element_type=F32)
        q_ref[:, c0:c0 + ct] = (_head_rms(q, pool) * (qg_ref[:, c0:c0 + ct] * q_scale)).astype(BF16)
        k = jnp.dot(hkv, wkv_ref[:, c0:c0 + ct], preferred_element_type=F32)
        k_ref[:, c0:c0 + ct] = (_head_rms(k, pool) * kg_ref[:, c0:c0 + ct]).astype(BF16)
        v_ref[:, c0:c0 + ct] = jnp.dot(hkv, wkv_ref[:, d + c0:d + c0 + ct],
                                       preferred_element_type=F32).astype(BF16)


def _qkv_proj(h, mods, kvmods, g_mix, g_kv, wq, wkv, q_gain, k_gain, seq, head_dim):
    t, d = h.shape
    tm = ROW_TILE
    per_seq = seq // tm
    out = jax.ShapeDtypeStruct((t, d), BF16)
    row_spec = pl.BlockSpec((tm, d), lambda i: (i, 0))
    return pl.pallas_call(
        functools.partial(_qkv_proj_kernel, head_dim=head_dim,
                          q_scale=head_dim ** -0.5 * math.log2(math.e)),
        grid=(t // tm,),
        in_specs=[
            row_spec,
            pl.BlockSpec((1,) + mods.shape[1:], lambda i: (i // per_seq, 0, 0)),
            pl.BlockSpec((1,) + kvmods.shape[1:], lambda i: (i // per_seq, 0, 0)),
            _resident((1, d)),
            _resident((1, d)),
            _resident((d, d)),
            _resident((d, 2 * d)),
            _resident((1, d)),
            _resident((1, d)),
        ],
        out_specs=[row_spec, row_spec, row_spec],
        out_shape=[out, out, out],
        compiler_params=_params("parallel"),
        name="qkv_proj",
    )(h, mods, kvmods, g_mix.reshape(1, d), g_kv.reshape(1, d), wq, wkv,
      jnp.tile(q_gain, d // head_dim).reshape(1, d), jnp.tile(k_gain, d // head_dim).reshape(1, d))


def _softplus2(z):
    return jnp.maximum(z, 0.0) + jnp.log2(1.0 + jnp.exp2(-jnp.abs(z)))


def _neg_suffix_sum(x, neg_upper):
    return jnp.dot(x.astype(BF16), neg_upper, preferred_element_type=F32)


def _attn_kernel(q_ref, k_ref, v_ref, o_ref, *, tk, head_dim):
    assert tk & (tk - 1) == 0 and tk % LANES == 0
    tq = 2 * tk
    n_heads = LANES // head_dim
    half_rows = n_heads * tk
    n_blocks = q_ref.shape[1] // tq
    lane = lax.broadcasted_iota(jnp.int32, (1, LANES), 1)
    head_lanes = [(lane >= h * head_dim) & (lane < (h + 1) * head_dim) for h in range(n_heads)]
    row = lax.broadcasted_iota(jnp.int32, (tk, tk), 0)
    col = lax.broadcasted_iota(jnp.int32, (tk, tk), 1)
    neg_upper = jnp.where(row > col, -1.0, 0.0).astype(BF16)
    row = lax.broadcasted_iota(jnp.int32, (half_rows, tk), 0)
    col = lax.broadcasted_iota(jnp.int32, (half_rows, tk), 1)
    diag_mask = col < (row & (tk - 1))

    def step(qs, k0, mask, acc, later):
        z = _mxu_nt(qs, k_ref[0, pl.ds(k0, tk), :])
        sp = _softplus2(z)
        if mask is not None:
            sp = jnp.where(mask, sp, 0.0)
        after = _neg_suffix_sum(sp, neg_upper)
        p = jnp.exp2(z - sp + after)
        if mask is not None:
            p = jnp.where(mask, p, 0.0)
        acc = acc + jnp.exp2(later) * _mxu(p, v_ref[0, pl.ds(k0, tk), :])
        return acc, later + (after[:, 0:1] - sp[:, 0:1])

    def q_block(qi, carry):
        q0 = pl.multiple_of(qi * tq, tq)
        q1 = pl.multiple_of(q0 + tk, tk)
        q_lo = q_ref[0, pl.ds(q0, tk), :]
        q_hi = q_ref[0, pl.ds(q1, tk), :]
        q_hi = jnp.concatenate([jnp.where(m, q_hi, jnp.zeros_like(q_hi)) for m in head_lanes], axis=0)
        q_lo = jnp.concatenate([jnp.where(m, q_lo, jnp.zeros_like(q_lo)) for m in head_lanes], axis=0)
        q_all = jnp.concatenate([q_lo, q_hi], axis=0)
        zeros = jnp.zeros((half_rows, LANES), F32)
        acc_hi, later_hi = step(q_hi, q1, diag_mask, zeros, zeros)
        acc_hi, later_hi = step(q_hi, q0, None, acc_hi, later_hi)
        acc_lo, later_lo = step(q_lo, q0, diag_mask, zeros, zeros)
        acc = jnp.concatenate([acc_lo, acc_hi], axis=0)
        later = jnp.concatenate([later_lo, later_hi], axis=0)

        def k_pair(j, st):
            k0 = pl.multiple_of((2 * qi - 1 - 2 * j) * tk, tk)
            st = step(q_all, k0, None, *st)
            return step(q_all, pl.multiple_of(k0 - tk, tk), None, *st)

        acc, _ = lax.fori_loop(0, qi, k_pair, (acc, later))
        for half, r0 in enumerate((q0, q1)):
            base = half * half_rows
            o = sum(jnp.where(m, acc[base + h * tk:base + (h + 1) * tk], 0.0)
                    for h, m in enumerate(head_lanes))
            o_ref[0, pl.ds(r0, tk), :] = o.astype(BF16)
        return carry

    lax.fori_loop(0, n_blocks, q_block, 0)


def _attention(q, k, v, head_dim):
    nb, s, d = q.shape
    spec = pl.BlockSpec((1, s, LANES), lambda b, p: (b, 0, p))
    return pl.pallas_call(
        functools.partial(_attn_kernel, tk=ATTN_TILE, head_dim=head_dim),
        grid=(nb, d // LANES),
        in_specs=[spec, spec, spec],
        out_specs=spec,
        out_shape=jax.ShapeDtypeStruct((nb, s, d), BF16),
        compiler_params=_params("parallel", "parallel"),
        name="sb_attention",
    )(q, k, v)


def kernel(x, c, positions, ada_w, ada_b, norm_mix_g, norm_ffn_g, ret_w_in, ret_w_out, kv_ada_w, kv_ada_b, kv_norm_g, w_kv, k_norm_g, sb_w_q, q_norm_g, sb_w_out, ffn_w_in, ffn_conv_w, ffn_conv_b, ffn_w_out):
    nb, s, d = x.shape
    t = nb * s
    depth = ada_w.shape[0]
    n_mod = ada_w.shape[2] // d
    dk = d // RET_HEADS
    dv = 2 * dk
    head_dim = d // SB_HEADS

    mods = _ada(c, ada_w, ada_b).reshape(depth, nb, n_mod, d)
    kvmods = _ada(c, kv_ada_w[None], kv_ada_b[None]).reshape(nb, 2, d)
    cos, sin = _rope_tables(positions, dk // 2)
    cos = cos.reshape(t, dk // 2)
    sin = sin.reshape(t, dk // 2)

    h = x.reshape(t, d)

    proj = _ret_proj(h, mods[0], norm_mix_g[0], ret_w_in[0].astype(BF16), cos, sin,
                     s, RET_HEADS * dk, dk ** -0.5)
    og = _retention(proj.reshape(nb, s, -1), RET_HEADS, dk, dv, RET_CHUNK)
    h = _out_proj(og.reshape(t, -1), ret_w_out[0].astype(BF16), h, mods[0], 2, s)
    act = _ffn_up(h, mods[0], norm_ffn_g[0], ffn_w_in[0].astype(BF16), ffn_conv_w[0], ffn_conv_b[0], s)
    h = _out_proj(act, ffn_w_out[0].astype(BF16), h, mods[0], 5, s)

    q, k, v = _qkv_proj(h, mods[1], kvmods, norm_mix_g[1], kv_norm_g, sb_w_q[0].astype(BF16),
                        w_kv.astype(BF16), q_norm_g[0], k_norm_g, s, head_dim)
    o = _attention(q.reshape(nb, s, d), k.reshape(nb, s, d), v.reshape(nb, s, d), head_dim)
    h = _out_proj(o.reshape(t, d), sb_w_out[0].astype(BF16), h, mods[1], 2, s)
    act = _ffn_up(h, mods[1], norm_ffn_g[1], ffn_w_in[1].astype(BF16), ffn_conv_w[1], ffn_conv_b[1], s)
    h = _out_proj(act, ffn_w_out[1].astype(BF16), h, mods[1], 5, s)
    return h.reshape(nb, s, d)
```

```python
import functools
import math

import jax
import jax.numpy as jnp
from jax import lax
from jax.experimental import pallas as pl
from jax.experimental.pallas import tpu as pltpu

F32 = jnp.float32
BF16 = jnp.bfloat16

EPS = 1e-6
ROPE_BASE = 10000.0
RET_HEADS = 4
SB_HEADS = 16
CONV_WIDTH = 3

LANES = 128
SUBLANES = 8
MXU_DIM = 256
V7X_VMEM_BYTES = 64 * 1024 * 1024
VMEM_LIMIT_BYTES = (V7X_VMEM_BYTES * 7) // 8

ROW_TILE = 256
OUT_ROW_TILE = 512
ATTN_TILE = 256
RET_BLOCK = MXU_DIM


def _params(*semantics):
    return pltpu.CompilerParams(dimension_semantics=semantics,
                                vmem_limit_bytes=VMEM_LIMIT_BYTES)


def _resident(shape):
    zeros = (0,) * len(shape)
    return pl.BlockSpec(shape, lambda *_: zeros, pipeline_mode=pl.Buffered(1))


def _per_seq(mods, tiles_per_seq):
    return pl.BlockSpec((1,) + mods.shape[1:], lambda i: (i // tiles_per_seq, 0, 0))


def _silu(x):
    h = 0.5 * x
    return h + h * jnp.tanh(h)


def _unit_rms(x):
    return x * lax.rsqrt(jnp.mean(x * x, axis=-1, keepdims=True) + EPS)


def _mxu(a, b):
    return jnp.dot(a.astype(BF16), b.astype(BF16), preferred_element_type=F32)


def _mxu_nt(a, b):
    return lax.dot_general(a.astype(BF16), b.astype(BF16), (((1,), (1,)), ((), ())),
                           preferred_element_type=F32)


def _mxu_tn(a, b):
    return lax.dot_general(a.astype(BF16), b.astype(BF16), (((0,), (0,)), ((), ())),
                           preferred_element_type=F32)


def _ada_kernel(c_ref, w_ref, b_ref, o_ref):
    o_ref[...] = _mxu(_silu(c_ref[...]), w_ref[...]) + b_ref[...]


def _ada(c, w, b, col_tile=2048):
    nb, d = c.shape
    nl, _, n = w.shape
    return pl.pallas_call(
        _ada_kernel,
        grid=(nl, n // col_tile),
        in_specs=[
            pl.BlockSpec((nb, d), lambda l, j: (0, 0)),
            pl.BlockSpec((None, d, col_tile), lambda l, j: (l, 0, j)),
            pl.BlockSpec((None, 1, col_tile), lambda l, j: (l, 0, j)),
        ],
        out_specs=pl.BlockSpec((None, nb, col_tile), lambda l, j: (l, 0, j)),
        out_shape=jax.ShapeDtypeStruct((nl, nb, n), F32),
        compiler_params=_params("parallel", "parallel"),
        name="ada_mods",
    )(c, w, b.reshape(nl, 1, n))


def _rope_kernel(pos_ref, cos_ref, sin_ref, *, half):
    i = lax.broadcasted_iota(jnp.int32, (1, half), 1).astype(F32)
    inv = jnp.exp(i * (-math.log(ROPE_BASE) / half))
    ang = pos_ref[0].astype(F32) * inv
    cos_ref[0] = jnp.cos(ang)
    sin_ref[0] = jnp.sin(ang)


def _rope_tables(positions, half):
    nb, s = positions.shape
    out = jax.ShapeDtypeStruct((nb, s, half), F32)
    return pl.pallas_call(
        functools.partial(_rope_kernel, half=half),
        grid=(nb,),
        in_specs=[pl.BlockSpec((1, s, 1), lambda b: (b, 0, 0))],
        out_specs=[pl.BlockSpec((1, s, half), lambda b: (b, 0, 0))] * 2,
        out_shape=[out, out],
        compiler_params=_params("parallel"),
        name="rope_tables",
    )(positions.reshape(nb, s, 1))


def _ret_proj_kernel(h_ref, mod_ref, g_ref, w_ref, cos_ref, sin_ref, o_ref,
                     *, qk_cols, gate_col, k_scale):
    mod = mod_ref[0]
    hn = (_unit_rms(h_ref[...]) * g_ref[...] * (1.0 + mod[1:2, :]) + mod[0:1, :]).astype(BF16)
    cos = cos_ref[...]
    sin = sin_ref[...]
    half = cos.shape[-1]
    n = w_ref.shape[1]
    for c in range(0, 2 * qk_cols, 2 * half):
        acc = jnp.dot(hn, w_ref[:, c:c + 2 * half], preferred_element_type=F32)
        x1 = acc[:, :half]
        x2 = acc[:, half:]
        sc = k_scale if c >= qk_cols else 1.0
        o_ref[:, c:c + half] = ((x1 * cos - x2 * sin) * sc).astype(BF16)
        o_ref[:, c + half:c + 2 * half] = ((x1 * sin + x2 * cos) * sc).astype(BF16)
    step = 2 * MXU_DIM
    for c in range(2 * qk_cols, n, step):
        acc = jnp.dot(hn, w_ref[:, c:c + step], preferred_element_type=F32)
        o_ref[:, c:c + step] = (_silu(acc) if c >= gate_col else acc).astype(BF16)


def _ret_proj(h, mods, gain, w, cos, sin, seq, qk_cols, gate_col, k_scale):
    t, d = h.shape
    n = w.shape[1]
    tm = ROW_TILE
    half = cos.shape[-1]
    return pl.pallas_call(
        functools.partial(_ret_proj_kernel, qk_cols=qk_cols, gate_col=gate_col, k_scale=k_scale),
        grid=(t // tm,),
        in_specs=[
            pl.BlockSpec((tm, d), lambda i: (i, 0)),
            _per_seq(mods, seq // tm),
            _resident((1, d)),
            _resident((d, n)),
            pl.BlockSpec((tm, half), lambda i: (i, 0)),
            pl.BlockSpec((tm, half), lambda i: (i, 0)),
        ],
        out_specs=pl.BlockSpec((tm, n), lambda i: (i, 0)),
        out_shape=jax.ShapeDtypeStruct((t, n), BF16),
        compiler_params=_params("parallel"),
        name="ret_proj",
    )(h, mods, gain.reshape(1, d), w, cos, sin)


def _retention_kernel(q_ref, k_ref, v_ref, g_ref, o_ref, state_ref, *, chunk):
    n_chunks = q_ref.shape[1] // chunk
    head = jnp.full((1, 1), pl.program_id(1), jnp.int32).astype(F32)
    log_gamma = jnp.log(1.0 - jnp.exp2(-5.0 - head))
    row = lax.broadcasted_iota(jnp.int32, (chunk, chunk), 0).astype(F32)
    col = lax.broadcasted_iota(jnp.int32, (chunk, chunk), 1).astype(F32)
    rel = row - col
    intra = jnp.where(rel >= 0, jnp.exp(jnp.maximum(rel, 0.0) * log_gamma), 0.0)
    idx = row[:, 0:1]
    q_decay = jnp.exp((idx + 1.0) * log_gamma)
    k_decay = jnp.exp((chunk - 1.0 - idx) * log_gamma)
    chunk_decay = jnp.exp(chunk * log_gamma)

    for n in range(n_chunks):
        rows = slice(n * chunk, (n + 1) * chunk)
        qc = q_ref[0, rows, :]
        kc = k_ref[0, rows, :]
        vc = v_ref[0, rows, :]
        o = _mxu(_mxu_nt(qc, kc) * intra, vc)
        if n > 0:
            o = o + _mxu(qc.astype(F32) * q_decay, state_ref[...])
        if n < n_chunks - 1:
            update = _mxu_tn(kc.astype(F32) * k_decay, vc)
            state_ref[...] = update if n == 0 else state_ref[...] * chunk_decay + update
        o_ref[0, rows, :] = (g_ref[0, rows, :].astype(F32) * _unit_rms(o)).astype(BF16)


def _retention(proj, heads, dk, dv):
    nb, s, _ = proj.shape
    kb0 = heads
    vb0 = 2 * heads * dk // dv
    gb0 = vb0 + heads
    return pl.pallas_call(
        functools.partial(_retention_kernel, chunk=RET_BLOCK),
        grid=(nb, heads),
        in_specs=[
            pl.BlockSpec((1, s, dk), lambda b, h: (b, 0, h)),
            pl.BlockSpec((1, s, dk), lambda b, h: (b, 0, kb0 + h)),
            pl.BlockSpec((1, s, dv), lambda b, h: (b, 0, vb0 + h)),
            pl.BlockSpec((1, s, dv), lambda b, h: (b, 0, gb0 + h)),
        ],
        out_specs=pl.BlockSpec((1, s, dv), lambda b, h: (b, 0, h)),
        out_shape=jax.ShapeDtypeStruct((nb, s, heads * dv), BF16),
        scratch_shapes=[pltpu.VMEM((dk, dv), F32)],
        compiler_params=_params("parallel", "parallel"),
        name="retention",
    )(proj, proj, proj, proj)


def _out_proj_kernel(*refs, gate_row, norm_rows):
    a_ref, w_ref, h_ref, mod_ref = refs[:4]
    n_norm = len(norm_rows)
    norm_in = refs[4:4 + 2 * n_norm]
    o_ref = refs[4 + 2 * n_norm]
    hn_refs = refs[5 + 2 * n_norm:]
    gate = mod_ref[0][gate_row:gate_row + 1, :]
    h = h_ref[...] + gate * jnp.dot(a_ref[...], w_ref[...], preferred_element_type=F32)
    o_ref[...] = h
    if n_norm:
        xn = _unit_rms(h)
        for i, (shift_row, scale_row) in enumerate(norm_rows):
            gain = norm_in[2 * i][...]
            mod = norm_in[2 * i + 1][0]
            hn_refs[i][...] = (xn * gain * (1.0 + mod[scale_row:scale_row + 1, :])
                               + mod[shift_row:shift_row + 1, :]).astype(BF16)


def _out_proj(a, w, h, mods, gate_row, seq, norms=()):
    t, k = a.shape
    d = w.shape[1]
    tm = OUT_ROW_TILE
    row_spec = pl.BlockSpec((tm, d), lambda i: (i, 0))
    in_specs = [pl.BlockSpec((tm, k), lambda i: (i, 0)), _resident((k, d)), row_spec,
                _per_seq(mods, seq // tm)]
    args = [a, w, h, mods]
    for gain, nmods, _, _ in norms:
        in_specs += [_resident((1, d)), _per_seq(nmods, seq // tm)]
        args += [gain.reshape(1, d), nmods]
    out = pl.pallas_call(
        functools.partial(_out_proj_kernel, gate_row=gate_row,
                          norm_rows=tuple((sh, sc) for _, _, sh, sc in norms)),
        grid=(t // tm,),
        in_specs=in_specs,
        out_specs=[row_spec] * (1 + len(norms)),
        out_shape=[jax.ShapeDtypeStruct((t, d), F32)]
        + [jax.ShapeDtypeStruct((t, d), BF16)] * len(norms),
        compiler_params=_params("parallel"),
        name="out_proj",
    )(*args)
    return out


def _ffn_up_kernel(hn_ref, w_ref, cw_ref, cb_ref, o_ref, carry_ref, buf_ref, *, per_seq, col_tile):
    tm = hn_ref.shape[0]
    f = o_ref.shape[1]
    halo = SUBLANES

    @pl.when(pl.program_id(0) % per_seq == 0)
    def _():
        carry_ref[...] = jnp.zeros_like(carry_ref)

    hn = hn_ref[...]

    def conv(c):
        u = jnp.dot(hn, w_ref[:, c:c + col_tile], preferred_element_type=F32)
        buf_ref[0:halo, :] = carry_ref[:, c:c + col_tile]
        buf_ref[halo:halo + tm, :] = u
        carry_ref[:, c:c + col_tile] = u[tm - halo:, :]
        y = cb_ref[:, c:c + col_tile] + buf_ref[halo - 2:halo - 2 + tm, :] * cw_ref[0:1, c:c + col_tile]
        y = y + buf_ref[halo - 1:halo - 1 + tm, :] * cw_ref[1:2, c:c + col_tile]
        return y + u * cw_ref[2:3, c:c + col_tile]

    for c in range(0, f, col_tile):
        val = conv(c)
        gate = conv(f + c)
        o_ref[:, c:c + col_tile] = (val * _silu(gate)).astype(BF16)


def _ffn_up(hn, w, conv_w, conv_b, seq):
    t, d = hn.shape
    f2 = w.shape[1]
    f = f2 // 2
    tm = ROW_TILE
    col_tile = MXU_DIM
    return pl.pallas_call(
        functools.partial(_ffn_up_kernel, per_seq=seq // tm, col_tile=col_tile),
        grid=(t // tm,),
        in_specs=[
            pl.BlockSpec((tm, d), lambda i: (i, 0)),
            _resident((d, f2)),
            _resident((CONV_WIDTH, f2)),
            _resident((1, f2)),
        ],
        out_specs=pl.BlockSpec((tm, f), lambda i: (i, 0)),
        out_shape=jax.ShapeDtypeStruct((t, f), BF16),
        scratch_shapes=[pltpu.VMEM((SUBLANES, f2), F32),
                        pltpu.VMEM((SUBLANES + tm, col_tile), F32)],
        compiler_params=_params("arbitrary"),
        name="ffn_up",
    )(hn, w, conv_w, conv_b.reshape(1, f2))


def _qkv_proj_kernel(hq_ref, hkv_ref, wq_ref, wkv_ref, qg_ref, kg_ref, q_ref, k_ref, v_ref,
                     *, head_dim, q_scale):
    tm, d = hq_ref.shape
    hq = hq_ref[...]
    hkv = hkv_ref[...]
    ct = MXU_DIM
    cols = range(0, d, ct)
    r = lax.broadcasted_iota(jnp.int32, (ct, ct), 0) // head_dim
    c = lax.broadcasted_iota(jnp.int32, (ct, ct), 1) // head_dim
    pool = jnp.where(r == c, 1.0 / head_dim, 0.0).astype(BF16)
    raw = [jnp.dot(hq, wq_ref[:, c0:c0 + ct], preferred_element_type=F32) for c0 in cols]
    raw += [jnp.dot(hkv, wkv_ref[:, c0:c0 + ct], preferred_element_type=F32) for c0 in cols]
    squares = jnp.concatenate([(x * x).astype(BF16) for x in raw], axis=0)
    inv_rms = lax.rsqrt(jnp.dot(squares, pool, preferred_element_type=F32) + EPS)
    for i, c0 in enumerate(cols):
        j = len(cols) + i
        q_ref[:, c0:c0 + ct] = (raw[i] * inv_rms[i * tm:(i + 1) * tm]
                                * (qg_ref[:, c0:c0 + ct] * q_scale)).astype(BF16)
        k_ref[:, c0:c0 + ct] = (raw[j] * inv_rms[j * tm:(j + 1) * tm]
                                * kg_ref[:, c0:c0 + ct]).astype(BF16)
    for c0 in cols:
        v_ref[:, c0:c0 + ct] = jnp.dot(hkv, wkv_ref[:, d + c0:d + c0 + ct],
                                       preferred_element_type=F32).astype(BF16)


def _qkv_proj(hq, hkv, wq, wkv, q_gain, k_gain, head_dim):
    t, d = hq.shape
    tm = ROW_TILE
    out = jax.ShapeDtypeStruct((t, d), BF16)
    row_spec = pl.BlockSpec((tm, d), lambda i: (i, 0))
    return pl.pallas_call(
        functools.partial(_qkv_proj_kernel, head_dim=head_dim,
                          q_scale=head_dim ** -0.5 * math.log2(math.e)),
        grid=(t // tm,),
        in_specs=[row_spec, row_spec, _resident((d, d)), _resident((d, 2 * d)),
                  _resident((1, d)), _resident((1, d))],
        out_specs=[row_spec, row_spec, row_spec],
        out_shape=[out, out, out],
        compiler_params=_params("parallel"),
        name="qkv_proj",
    )(hq, hkv, wq, wkv,
      jnp.tile(q_gain, d // head_dim).reshape(1, d), jnp.tile(k_gain, d // head_dim).reshape(1, d))


def _softplus2(z):
    return jnp.maximum(z, 0.0) + jnp.log2(1.0 + jnp.exp2(-jnp.abs(z)))


def _neg_suffix_sum(x, neg_upper):
    return jnp.dot(x.astype(BF16), neg_upper, preferred_element_type=F32)


def _attn_kernel(q_ref, k_ref, v_ref, o_ref, *, tk, head_dim):
    assert tk & (tk - 1) == 0 and tk % LANES == 0
    tq = 2 * tk
    n_heads = LANES // head_dim
    half_rows = n_heads * tk
    n_blocks = q_ref.shape[1] // tq
    lane = lax.broadcasted_iota(jnp.int32, (1, LANES), 1)
    head_lanes = [(lane >= h * head_dim) & (lane < (h + 1) * head_dim) for h in range(n_heads)]
    row = lax.broadcasted_iota(jnp.int32, (tk, tk), 0)
    col = lax.broadcasted_iota(jnp.int32, (tk, tk), 1)
    neg_upper = jnp.where(row > col, -1.0, 0.0).astype(BF16)
    row = lax.broadcasted_iota(jnp.int32, (half_rows, tk), 0)
    col = lax.broadcasted_iota(jnp.int32, (half_rows, tk), 1)
    diag_mask = col < (row & (tk - 1))

    def step(qs, k0, mask, acc, later):
        z = _mxu_nt(qs, k_ref[0, pl.ds(k0, tk), :])
        sp = _softplus2(z)
        if mask is not None:
            sp = jnp.where(mask, sp, 0.0)
        after = _neg_suffix_sum(sp, neg_upper)
        p = jnp.exp2(z - sp + after)
        if mask is not None:
            p = jnp.where(mask, p, 0.0)
        acc = acc + jnp.exp2(later) * _mxu(p, v_ref[0, pl.ds(k0, tk), :])
        return acc, later + (after[:, 0:1] - sp[:, 0:1])

    def q_block(qi, carry):
        q0 = pl.multiple_of(qi * tq, tq)
        q1 = pl.multiple_of(q0 + tk, tk)
        q_lo = q_ref[0, pl.ds(q0, tk), :]
        q_hi = q_ref[0, pl.ds(q1, tk), :]
        q_hi = jnp.concatenate([jnp.where(m, q_hi, jnp.zeros_like(q_hi)) for m in head_lanes], axis=0)
        q_lo = jnp.concatenate([jnp.where(m, q_lo, jnp.zeros_like(q_lo)) for m in head_lanes], axis=0)
        q_all = jnp.concatenate([q_lo, q_hi], axis=0)
        zeros = jnp.zeros((half_rows, LANES), F32)
        acc_hi, later_hi = step(q_hi, q1, diag_mask, zeros, zeros)
        acc_hi, later_hi = step(q_hi, q0, None, acc_hi, later_hi)
        acc_lo, later_lo = step(q_lo, q0, diag_mask, zeros, zeros)
        acc = jnp.concatenate([acc_lo, acc_hi], axis=0)
        later = jnp.concatenate([later_lo, later_hi], axis=0)

        def k_pair(j, st):
            k0 = pl.multiple_of((2 * qi - 1 - 2 * j) * tk, tk)
            st = step(q_all, k0, None, *st)
            return step(q_all, pl.multiple_of(k0 - tk, tk), None, *st)

        acc, _ = lax.fori_loop(0, qi, k_pair, (acc, later))
        for half, r0 in enumerate((q0, q1)):
            base = half * half_rows
            o = sum(jnp.where(m, acc[base + h * tk:base + (h + 1) * tk], 0.0)
                    for h, m in enumerate(head_lanes))
            o_ref[0, pl.ds(r0, tk), :] = o.astype(BF16)
        return carry

    lax.fori_loop(0, n_blocks, q_block, 0)


def _attention(q, k, v, head_dim):
    nb, s, d = q.shape
    spec = pl.BlockSpec((1, s, LANES), lambda b, p: (b, 0, p))
    return pl.pallas_call(
        functools.partial(_attn_kernel, tk=ATTN_TILE, head_dim=head_dim),
        grid=(nb, d // LANES),
        in_specs=[spec, spec, spec],
        out_specs=spec,
        out_shape=jax.ShapeDtypeStruct((nb, s, d), BF16),
        compiler_params=_params("parallel", "parallel"),
        name="sb_attention",
    )(q, k, v)


def kernel(x, c, positions, ada_w, ada_b, norm_mix_g, norm_ffn_g, ret_w_in, ret_w_out, kv_ada_w, kv_ada_b, kv_norm_g, w_kv, k_norm_g, sb_w_q, q_norm_g, sb_w_out, ffn_w_in, ffn_conv_w, ffn_conv_b, ffn_w_out):
    nb, s, d = x.shape
    t = nb * s
    depth = ada_w.shape[0]
    n_mod = ada_w.shape[2] // d
    dk = d // RET_HEADS
    dv = 2 * dk
    head_dim = d // SB_HEADS
    shift_m, scale_m, gate_m, shift_f, scale_f, gate_f = range(6)

    mods = _ada(c, ada_w, ada_b).reshape(depth, nb, n_mod, d)
    kvmods = _ada(c, kv_ada_w[None], kv_ada_b[None]).reshape(nb, 2, d)
    cos, sin = _rope_tables(positions, dk // 2)
    cos = cos.reshape(t, dk // 2)
    sin = sin.reshape(t, dk // 2)

    h = x.reshape(t, d)

    proj = _ret_proj(h, mods[0], norm_mix_g[0], ret_w_in[0].astype(BF16), cos, sin,
                     s, RET_HEADS * dk, 2 * RET_HEADS * dk + RET_HEADS * dv, dk ** -0.5)
    og = _retention(proj.reshape(nb, s, -1), RET_HEADS, dk, dv)
    h, hf = _out_proj(og.reshape(t, -1), ret_w_out[0].astype(BF16), h, mods[0], gate_m, s,
                      norms=[(norm_ffn_g[0], mods[0], shift_f, scale_f)])
    act = _ffn_up(hf, ffn_w_in[0].astype(BF16), ffn_conv_w[0], ffn_conv_b[0], s)
    h, hq, hkv = _out_proj(act, ffn_w_out[0].astype(BF16), h, mods[0], gate_f, s,
                           norms=[(norm_mix_g[1], mods[1], shift_m, scale_m),
                                  (kv_norm_g, kvmods, 0, 1)])

    q, k, v = _qkv_proj(hq, hkv, sb_w_q[0].astype(BF16), w_kv.astype(BF16),
                        q_norm_g[0], k_norm_g, head_dim)
    o = _attention(q.reshape(nb, s, d), k.reshape(nb, s, d), v.reshape(nb, s, d), head_dim)
    h, hf = _out_proj(o.reshape(t, d), sb_w_out[0].astype(BF16), h, mods[1], gate_m, s,
                      norms=[(norm_ffn_g[1], mods[1], shift_f, scale_f)])
    act = _ffn_up(hf, ffn_w_in[1].astype(BF16), ffn_conv_w[1], ffn_conv_b[1], s)
    (h,) = _out_proj(act, ffn_w_out[1].astype(BF16), h, mods[1], gate_f, s)
    return h.reshape(nb, s, d)
```

```python
import functools
import math

import jax
import jax.numpy as jnp
from jax import lax
from jax.experimental import pallas as pl
from jax.experimental.pallas import tpu as pltpu

F32 = jnp.float32
BF16 = jnp.bfloat16

EPS = 1e-6
ROPE_BASE = 10000.0
RET_HEADS = 4
SB_HEADS = 16
CONV_WIDTH = 3

LANES = 128
SUBLANES = 8
MXU_DIM = 256
V7X_VMEM_BYTES = 64 * 1024 * 1024
VMEM_LIMIT_BYTES = (V7X_VMEM_BYTES * 7) // 8

ROW_TILE = 512
OUT_ROW_TILE = 512
ATTN_TILE = 256
RET_BLOCK = MXU_DIM


def _params(*semantics):
    return pltpu.CompilerParams(dimension_semantics=semantics,
                                vmem_limit_bytes=VMEM_LIMIT_BYTES)


def _resident(shape):
    zeros = (0,) * len(shape)
    return pl.BlockSpec(shape, lambda *_: zeros, pipeline_mode=pl.Buffered(1))


def _per_seq(mods, tiles_per_seq):
    return pl.BlockSpec((1,) + mods.shape[1:], lambda i: (i // tiles_per_seq, 0, 0))


def _silu(x):
    h = 0.5 * x
    return h + h * jnp.tanh(h)


def _unit_rms(x):
    return x * lax.rsqrt(jnp.mean(x * x, axis=-1, keepdims=True) + EPS)


def _mxu(a, b):
    return jnp.dot(a.astype(BF16), b.astype(BF16), preferred_element_type=F32)


def _mxu_nt(a, b):
    return lax.dot_general(a.astype(BF16), b.astype(BF16), (((1,), (1,)), ((), ())),
                           preferred_element_type=F32)


def _mxu_tn(a, b):
    return lax.dot_general(a.astype(BF16), b.astype(BF16), (((0,), (0,)), ((), ())),
                           preferred_element_type=F32)


def _ada_kernel(c_ref, w_ref, b_ref, o_ref):
    o_ref[...] = _mxu(_silu(c_ref[...]), w_ref[...]) + b_ref[...]


def _ada(c, w, b, col_tile=2048):
    nb, d = c.shape
    nl, _, n = w.shape
    return pl.pallas_call(
        _ada_kernel,
        grid=(nl, n // col_tile),
        in_specs=[
            pl.BlockSpec((nb, d), lambda l, j: (0, 0)),
            pl.BlockSpec((None, d, col_tile), lambda l, j: (l, 0, j)),
            pl.BlockSpec((None, 1, col_tile), lambda l, j: (l, 0, j)),
        ],
        out_specs=pl.BlockSpec((None, nb, col_tile), lambda l, j: (l, 0, j)),
        out_shape=jax.ShapeDtypeStruct((nl, nb, n), F32),
        compiler_params=_params("parallel", "parallel"),
        name="ada_mods",
    )(c, w, b.reshape(nl, 1, n))


def _rope_kernel(pos_ref, cos_ref, sin_ref, *, half):
    i = lax.broadcasted_iota(jnp.int32, (1, half), 1).astype(F32)
    inv = jnp.exp(i * (-math.log(ROPE_BASE) / half))
    ang = pos_ref[0].astype(F32) * inv
    cos_ref[0] = jnp.cos(ang)
    sin_ref[0] = jnp.sin(ang)


def _rope_tables(positions, half):
    nb, s = positions.shape
    out = jax.ShapeDtypeStruct((nb, s, half), F32)
    return pl.pallas_call(
        functools.partial(_rope_kernel, half=half),
        grid=(nb,),
        in_specs=[pl.BlockSpec((1, s, 1), lambda b: (b, 0, 0))],
        out_specs=[pl.BlockSpec((1, s, half), lambda b: (b, 0, 0))] * 2,
        out_shape=[out, out],
        compiler_params=_params("parallel"),
        name="rope_tables",
    )(positions.reshape(nb, s, 1))


def _ret_proj_kernel(h_ref, mod_ref, g_ref, w_ref, cos_ref, sin_ref, o_ref,
                     *, qk_cols, gate_col, k_scale):
    mod = mod_ref[0]
    hn = (_unit_rms(h_ref[...]) * g_ref[...] * (1.0 + mod[1:2, :]) + mod[0:1, :]).astype(BF16)
    cos = cos_ref[...]
    sin = sin_ref[...]
    half = cos.shape[-1]
    n = w_ref.shape[1]
    for c in range(0, 2 * qk_cols, 2 * half):
        acc = jnp.dot(hn, w_ref[:, c:c + 2 * half], preferred_element_type=F32)
        x1 = acc[:, :half]
        x2 = acc[:, half:]
        sc = k_scale if c >= qk_cols else 1.0
        o_ref[:, c:c + half] = ((x1 * cos - x2 * sin) * sc).astype(BF16)
        o_ref[:, c + half:c + 2 * half] = ((x1 * sin + x2 * cos) * sc).astype(BF16)
    step = 2 * MXU_DIM
    for c in range(2 * qk_cols, n, step):
        acc = jnp.dot(hn, w_ref[:, c:c + step], preferred_element_type=F32)
        o_ref[:, c:c + step] = (_silu(acc) if c >= gate_col else acc).astype(BF16)


def _ret_proj(h, mods, gain, w, cos, sin, seq, qk_cols, gate_col, k_scale):
    t, d = h.shape
    n = w.shape[1]
    tm = ROW_TILE
    half = cos.shape[-1]
    return pl.pallas_call(
        functools.partial(_ret_proj_kernel, qk_cols=qk_cols, gate_col=gate_col, k_scale=k_scale),
        grid=(t // tm,),
        in_specs=[
            pl.BlockSpec((tm, d), lambda i: (i, 0)),
            _per_seq(mods, seq // tm),
            _resident((1, d)),
            _resident((d, n)),
            pl.BlockSpec((tm, half), lambda i: (i, 0)),
            pl.BlockSpec((tm, half), lambda i: (i, 0)),
        ],
        out_specs=pl.BlockSpec((tm, n), lambda i: (i, 0)),
        out_shape=jax.ShapeDtypeStruct((t, n), BF16),
        compiler_params=_params("parallel"),
        name="ret_proj",
    )(h, mods, gain.reshape(1, d), w, cos, sin)


def _retention_kernel(q_ref, k_ref, v_ref, g_ref, o_ref, state_ref, *, chunk):
    n_chunks = q_ref.shape[1] // chunk
    head = jnp.full((1, 1), pl.program_id(1), jnp.int32).astype(F32)
    log_gamma = jnp.log(1.0 - jnp.exp2(-5.0 - head))
    row = lax.broadcasted_iota(jnp.int32, (chunk, chunk), 0).astype(F32)
    col = lax.broadcasted_iota(jnp.int32, (chunk, chunk), 1).astype(F32)
    rel = row - col
    intra = jnp.where(rel >= 0, jnp.exp(jnp.maximum(rel, 0.0) * log_gamma), 0.0)
    idx = row[:, 0:1]
    q_decay = jnp.exp((idx + 1.0) * log_gamma)
    k_decay = jnp.exp((chunk - 1.0 - idx) * log_gamma)
    chunk_decay = jnp.exp(chunk * log_gamma)

    for n in range(n_chunks):
        rows = slice(n * chunk, (n + 1) * chunk)
        qc = q_ref[0, rows, :]
        kc = k_ref[0, rows, :]
        vc = v_ref[0, rows, :]
        o = _mxu(_mxu_nt(qc, kc) * intra, vc)
        if n > 0:
            o = o + _mxu(qc.astype(F32) * q_decay, state_ref[...])
        if n < n_chunks - 1:
            update = _mxu_tn(kc.astype(F32) * k_decay, vc)
            state_ref[...] = update if n == 0 else state_ref[...] * chunk_decay + update
        o_ref[0, rows, :] = (g_ref[0, rows, :].astype(F32) * _unit_rms(o)).astype(BF16)


def _retention(proj, heads, dk, dv):
    nb, s, _ = proj.shape
    kb0 = heads
    vb0 = 2 * heads * dk // dv
    gb0 = vb0 + heads
    return pl.pallas_call(
        functools.partial(_retention_kernel, chunk=RET_BLOCK),
        grid=(nb, heads),
        in_specs=[
            pl.BlockSpec((1, s, dk), lambda b, h: (b, 0, h)),
            pl.BlockSpec((1, s, dk), lambda b, h: (b, 0, kb0 + h)),
            pl.BlockSpec((1, s, dv), lambda b, h: (b, 0, vb0 + h)),
            pl.BlockSpec((1, s, dv), lambda b, h: (b, 0, gb0 + h)),
        ],
        out_specs=pl.BlockSpec((1, s, dv), lambda b, h: (b, 0, h)),
        out_shape=jax.ShapeDtypeStruct((nb, s, heads * dv), BF16),
        scratch_shapes=[pltpu.VMEM((dk, dv), F32)],
        compiler_params=_params("parallel", "parallel"),
        name="retention",
    )(proj, proj, proj, proj)


def _out_proj_kernel(*refs, gate_row, norm_rows, sub_rows):
    a_ref, w_ref, h_ref, mod_ref = refs[:4]
    n_norm = len(norm_rows)
    norm_in = refs[4:4 + 2 * n_norm]
    o_ref = refs[4 + 2 * n_norm]
    hn_refs = refs[5 + 2 * n_norm:]
    gate = mod_ref[0][gate_row:gate_row + 1, :]
    for r0 in range(0, a_ref.shape[0], sub_rows):
        rows = slice(r0, r0 + sub_rows)
        h = h_ref[rows, :] + gate * jnp.dot(a_ref[rows, :], w_ref[...], preferred_element_type=F32)
        o_ref[rows, :] = h
        if n_norm:
            xn = _unit_rms(h)
            for i, (shift_row, scale_row) in enumerate(norm_rows):
                gain = norm_in[2 * i][...]
                mod = norm_in[2 * i + 1][0]
                hn_refs[i][rows, :] = (xn * gain * (1.0 + mod[scale_row:scale_row + 1, :])
                                       + mod[shift_row:shift_row + 1, :]).astype(BF16)


def _out_proj(a, w, h, mods, gate_row, seq, norms=()):
    t, k = a.shape
    d = w.shape[1]
    tm = OUT_ROW_TILE
    row_spec = pl.BlockSpec((tm, d), lambda i: (i, 0))
    in_specs = [pl.BlockSpec((tm, k), lambda i: (i, 0)), _resident((k, d)), row_spec,
                _per_seq(mods, seq // tm)]
    args = [a, w, h, mods]
    for gain, nmods, _, _ in norms:
        in_specs += [_resident((1, d)), _per_seq(nmods, seq // tm)]
        args += [gain.reshape(1, d), nmods]
    out = pl.pallas_call(
        functools.partial(_out_proj_kernel, gate_row=gate_row, sub_rows=MXU_DIM,
                          norm_rows=tuple((sh, sc) for _, _, sh, sc in norms)),
        grid=(t // tm,),
        in_specs=in_specs,
        out_specs=[row_spec] * (1 + len(norms)),
        out_shape=[jax.ShapeDtypeStruct((t, d), F32)]
        + [jax.ShapeDtypeStruct((t, d), BF16)] * len(norms),
        compiler_params=_params("parallel"),
        name="out_proj",
    )(*args)
    return out


def _ffn_up_kernel(hn_ref, w_ref, cw_ref, cb_ref, o_ref, carry_ref, buf_ref, *, per_seq, col_tile):
    tm = hn_ref.shape[0]
    f = o_ref.shape[1]
    halo = SUBLANES

    @pl.when(pl.program_id(0) % per_seq == 0)
    def _():
        carry_ref[...] = jnp.zeros_like(carry_ref)

    hn = hn_ref[...]

    def conv(c):
        u = jnp.dot(hn, w_ref[:, c:c + col_tile], preferred_element_type=F32)
        buf_ref[0:halo, :] = carry_ref[:, c:c + col_tile]
        buf_ref[halo:halo + tm, :] = u
        carry_ref[:, c:c + col_tile] = u[tm - halo:, :]
        y = cb_ref[:, c:c + col_tile] + buf_ref[halo - 2:halo - 2 + tm, :] * cw_ref[0:1, c:c + col_tile]
        y = y + buf_ref[halo - 1:halo - 1 + tm, :] * cw_ref[1:2, c:c + col_tile]
        return y + u * cw_ref[2:3, c:c + col_tile]

    for c in range(0, f, col_tile):
        val = conv(c)
        gate = conv(f + c)
        o_ref[:, c:c + col_tile] = (val * _silu(gate)).astype(BF16)


def _ffn_up(hn, w, conv_w, conv_b, seq):
    t, d = hn.shape
    f2 = w.shape[1]
    f = f2 // 2
    tm = ROW_TILE
    col_tile = MXU_DIM
    return pl.pallas_call(
        functools.partial(_ffn_up_kernel, per_seq=seq // tm, col_tile=col_tile),
        grid=(t // tm,),
        in_specs=[
            pl.BlockSpec((tm, d), lambda i: (i, 0)),
            _resident((d, f2)),
            _resident((CONV_WIDTH, f2)),
            _resident((1, f2)),
        ],
        out_specs=pl.BlockSpec((tm, f), lambda i: (i, 0)),
        out_shape=jax.ShapeDtypeStruct((t, f), BF16),
        scratch_shapes=[pltpu.VMEM((SUBLANES, f2), F32),
                        pltpu.VMEM((SUBLANES + tm, col_tile), F32)],
        compiler_params=_params("arbitrary"),
        name="ffn_up",
    )(hn, w, conv_w, conv_b.reshape(1, f2))


def _qkv_proj_kernel(hq_ref, hkv_ref, wq_ref, wkv_ref, qg_ref, kg_ref, q_ref, k_ref, v_ref,
                     *, head_dim, q_scale):
    tm, d = hq_ref.shape
    hq = hq_ref[...]
    hkv = hkv_ref[...]
    ct = MXU_DIM
    cols = range(0, d, ct)
    r = lax.broadcasted_iota(jnp.int32, (ct, ct), 0) // head_dim
    c = lax.broadcasted_iota(jnp.int32, (ct, ct), 1) // head_dim
    pool = jnp.where(r == c, 1.0 / head_dim, 0.0).astype(BF16)
    raw = [jnp.dot(hq, wq_ref[:, c0:c0 + ct], preferred_element_type=F32) for c0 in cols]
    raw += [jnp.dot(hkv, wkv_ref[:, c0:c0 + ct], preferred_element_type=F32) for c0 in cols]
    squares = jnp.concatenate([(x * x).astype(BF16) for x in raw], axis=0)
    inv_rms = lax.rsqrt(jnp.dot(squares, pool, preferred_element_type=F32) + EPS)
    for i, c0 in enumerate(cols):
        j = len(cols) + i
        q_ref[:, c0:c0 + ct] = (raw[i] * inv_rms[i * tm:(i + 1) * tm]
                                * (qg_ref[:, c0:c0 + ct] * q_scale)).astype(BF16)
        k_ref[:, c0:c0 + ct] = (raw[j] * inv_rms[j * tm:(j + 1) * tm]
                                * kg_ref[:, c0:c0 + ct]).astype(BF16)
    for c0 in cols:
        v_ref[:, c0:c0 + ct] = jnp.dot(hkv, wkv_ref[:, d + c0:d + c0 + ct],
                                       preferred_element_type=F32).astype(BF16)


def _qkv_proj(hq, hkv, wq, wkv, q_gain, k_gain, head_dim):
    t, d = hq.shape
    tm = ROW_TILE
    out = jax.ShapeDtypeStruct((t, d), BF16)
    row_spec = pl.BlockSpec((tm, d), lambda i: (i, 0))
    return pl.pallas_call(
        functools.partial(_qkv_proj_kernel, head_dim=head_dim,
                          q_scale=head_dim ** -0.5 * math.log2(math.e)),
        grid=(t // tm,),
        in_specs=[row_spec, row_spec, _resident((d, d)), _resident((d, 2 * d)),
                  _resident((1, d)), _resident((1, d))],
        out_specs=[row_spec, row_spec, row_spec],
        out_shape=[out, out, out],
        compiler_params=_params("parallel"),
        name="qkv_proj",
    )(hq, hkv, wq, wkv,
      jnp.tile(q_gain, d // head_dim).reshape(1, d), jnp.tile(k_gain, d // head_dim).reshape(1, d))


def _softplus2(z):
    return jnp.maximum(z, 0.0) + jnp.log2(1.0 + jnp.exp2(-jnp.abs(z)))


def _neg_suffix_sum(x, neg_upper):
    return jnp.dot(x.astype(BF16), neg_upper, preferred_element_type=F32)


def _attn_kernel(q_ref, k_ref, v_ref, o_ref, *, tk, head_dim):
    assert tk & (tk - 1) == 0 and tk % LANES == 0
    tq = 2 * tk
    n_heads = LANES // head_dim
    half_rows = n_heads * tk
    n_blocks = q_ref.shape[1] // tq
    lane = lax.broadcasted_iota(jnp.int32, (1, LANES), 1)
    head_lanes = [(lane >= h * head_dim) & (lane < (h + 1) * head_dim) for h in range(n_heads)]
    row = lax.broadcasted_iota(jnp.int32, (tk, tk), 0)
    col = lax.broadcasted_iota(jnp.int32, (tk, tk), 1)
    neg_upper = jnp.where(row > col, -1.0, 0.0).astype(BF16)
    row = lax.broadcasted_iota(jnp.int32, (half_rows, tk), 0)
    col = lax.broadcasted_iota(jnp.int32, (half_rows, tk), 1)
    diag_mask = col < (row & (tk - 1))

    def scores(qs, k0):
        return _mxu_nt(qs, k_ref[0, pl.ds(k0, tk), :])

    def consume(z, k0, mask, acc, later):
        sp = _softplus2(z)
        if mask is not None:
            sp = jnp.where(mask, sp, 0.0)
        after = _neg_suffix_sum(sp, neg_upper)
        p = jnp.exp2(z - sp + after)
        if mask is not None:
            p = jnp.where(mask, p, 0.0)
        acc = acc + jnp.exp2(later) * _mxu(p, v_ref[0, pl.ds(k0, tk), :])
        return acc, later + (after[:, 0:1] - sp[:, 0:1])

    def step(qs, k0, mask, acc, later):
        return consume(scores(qs, k0), k0, mask, acc, later)

    for qi in range(n_blocks):
        q0 = qi * tq
        q1 = q0 + tk
        q_lo = q_ref[0, q0:q0 + tk, :]
        q_hi = q_ref[0, q1:q1 + tk, :]
        q_hi = jnp.concatenate([jnp.where(m, q_hi, jnp.zeros_like(q_hi)) for m in head_lanes], axis=0)
        q_lo = jnp.concatenate([jnp.where(m, q_lo, jnp.zeros_like(q_lo)) for m in head_lanes], axis=0)
        q_all = jnp.concatenate([q_lo, q_hi], axis=0)
        zeros = jnp.zeros((half_rows, LANES), F32)
        acc_hi, later_hi = step(q_hi, q1, diag_mask, zeros, zeros)
        acc_hi, later_hi = step(q_hi, q0, None, acc_hi, later_hi)
        acc_lo, later_lo = step(q_lo, q0, diag_mask, zeros, zeros)
        acc = jnp.concatenate([acc_lo, acc_hi], axis=0)
        later = jnp.concatenate([later_lo, later_hi], axis=0)
        for kb in range(2 * qi - 1, -1, -1):
            acc, later = step(q_all, kb * tk, None, acc, later)
        for half, r0 in enumerate((q0, q1)):
            base = half * half_rows
            o = sum(jnp.where(m, acc[base + h * tk:base + (h + 1) * tk], 0.0)
                    for h, m in enumerate(head_lanes))
            o_ref[0, r0:r0 + tk, :] = o.astype(BF16)


def _attention(q, k, v, head_dim):
    nb, s, d = q.shape
    spec = pl.BlockSpec((1, s, LANES), lambda b, p: (b, 0, p))
    return pl.pallas_call(
        functools.partial(_attn_kernel, tk=ATTN_TILE, head_dim=head_dim),
        grid=(nb, d // LANES),
        in_specs=[spec, spec, spec],
        out_specs=spec,
        out_shape=jax.ShapeDtypeStruct((nb, s, d), BF16),
        compiler_params=_params("parallel", "parallel"),
        name="sb_attention",
    )(q, k, v)


def kernel(x, c, positions, ada_w, ada_b, norm_mix_g, norm_ffn_g, ret_w_in, ret_w_out, kv_ada_w, kv_ada_b, kv_norm_g, w_kv, k_norm_g, sb_w_q, q_norm_g, sb_w_out, ffn_w_in, ffn_conv_w, ffn_conv_b, ffn_w_out):
    nb, s, d = x.shape
    t = nb * s
    depth = ada_w.shape[0]
    n_mod = ada_w.shape[2] // d
    dk = d // RET_HEADS
    dv = 2 * dk
    head_dim = d // SB_HEADS
    shift_m, scale_m, gate_m, shift_f, scale_f, gate_f = range(6)

    mods = _ada(c, ada_w, ada_b).reshape(depth, nb, n_mod, d)
    kvmods = _ada(c, kv_ada_w[None], kv_ada_b[None]).reshape(nb, 2, d)
    cos, sin = _rope_tables(positions, dk // 2)
    cos = cos.reshape(t, dk // 2)
    sin = sin.reshape(t, dk // 2)

    h = x.reshape(t, d)

    proj = _ret_proj(h, mods[0], norm_mix_g[0], ret_w_in[0].astype(BF16), cos, sin,
                     s, RET_HEADS * dk, 2 * RET_HEADS * dk + RET_HEADS * dv, dk ** -0.5)
    og = _retention(proj.reshape(nb, s, -1), RET_HEADS, dk, dv)
    h, hf = _out_proj(og.reshape(t, -1), ret_w_out[0].astype(BF16), h, mods[0], gate_m, s,
                      norms=[(norm_ffn_g[0], mods[0], shift_f, scale_f)])
    act = _ffn_up(hf, ffn_w_in[0].astype(BF16), ffn_conv_w[0], ffn_conv_b[0], s)
    h, hq, hkv = _out_proj(act, ffn_w_out[0].astype(BF16), h, mods[0], gate_f, s,
                           norms=[(norm_mix_g[1], mods[1], shift_m, scale_m),
                                  (kv_norm_g, kvmods, 0, 1)])

    q, k, v = _qkv_proj(hq, hkv, sb_w_q[0].astype(BF16), w_kv.astype(BF16),
                        q_norm_g[0], k_norm_g, head_dim)
    o = _attention(q.reshape(nb, s, d), k.reshape(nb, s, d), v.reshape(nb, s, d), head_dim)
    h, hf = _out_proj(o.reshape(t, d), sb_w_out[0].astype(BF16), h, mods[1], gate_m, s,
                      norms=[(norm_ffn_g[1], mods[1], shift_f, scale_f)])
    act = _ffn_up(hf, ffn_w_in[1].astype(BF16), ffn_conv_w[1], ffn_conv_b[1], s)
    (h,) = _out_proj(act, ffn_w_out[1].astype(BF16), h, mods[1], gate_f, s)
    return h.reshape(nb, s, d)
```

```python
import functools
import math

import jax
import jax.numpy as jnp
from jax import lax
from jax.experimental import pallas as pl
from jax.experimental.pallas import tpu as pltpu

F32 = jnp.float32
BF16 = jnp.bfloat16

EPS = 1e-6
ROPE_BASE = 10000.0
RET_HEADS = 4
SB_HEADS = 16
CONV_WIDTH = 3

LANES = 128
SUBLANES = 8
MXU_DIM = 256
V7X_VMEM_BYTES = 64 * 1024 * 1024
VMEM_LIMIT_BYTES = (V7X_VMEM_BYTES * 7) // 8

ROW_TILE = 512
OUT_ROW_TILE = 1024
ATTN_TILE = 256
RET_BLOCK = MXU_DIM


def _params(*semantics):
    return pltpu.CompilerParams(dimension_semantics=semantics,
                                vmem_limit_bytes=VMEM_LIMIT_BYTES)


def _resident(shape):
    zeros = (0,) * len(shape)
    return pl.BlockSpec(shape, lambda *_: zeros, pipeline_mode=pl.Buffered(1))


def _per_seq(mods, tiles_per_seq):
    return pl.BlockSpec((1,) + mods.shape[1:], lambda i: (i // tiles_per_seq, 0, 0))


def _silu(x):
    h = 0.5 * x
    return h + h * jnp.tanh(h)


def _unit_rms(x):
    return x * lax.rsqrt(jnp.mean(x * x, axis=-1, keepdims=True) + EPS)


def _mxu(a, b):
    return jnp.dot(a.astype(BF16), b.astype(BF16), preferred_element_type=F32)


def _mxu_nt(a, b):
    return lax.dot_general(a.astype(BF16), b.astype(BF16), (((1,), (1,)), ((), ())),
                           preferred_element_type=F32)


def _mxu_tn(a, b):
    return lax.dot_general(a.astype(BF16), b.astype(BF16), (((0,), (0,)), ((), ())),
                           preferred_element_type=F32)


def _ada_kernel(c_ref, w_ref, b_ref, o_ref):
    o_ref[...] = _mxu(_silu(c_ref[...]), w_ref[...]) + b_ref[...]


def _ada(c, w, b, col_tile=2048):
    nb, d = c.shape
    nl, _, n = w.shape
    return pl.pallas_call(
        _ada_kernel,
        grid=(nl, n // col_tile),
        in_specs=[
            pl.BlockSpec((nb, d), lambda l, j: (0, 0)),
            pl.BlockSpec((None, d, col_tile), lambda l, j: (l, 0, j)),
            pl.BlockSpec((None, 1, col_tile), lambda l, j: (l, 0, j)),
        ],
        out_specs=pl.BlockSpec((None, nb, col_tile), lambda l, j: (l, 0, j)),
        out_shape=jax.ShapeDtypeStruct((nl, nb, n), F32),
        compiler_params=_params("parallel", "parallel"),
        name="ada_mods",
    )(c, w, b.reshape(nl, 1, n))


def _rope_kernel(pos_ref, cos_ref, sin_ref, *, half):
    i = lax.broadcasted_iota(jnp.int32, (1, half), 1).astype(F32)
    inv = jnp.exp(i * (-math.log(ROPE_BASE) / half))
    ang = pos_ref[0].astype(F32) * inv
    cos_ref[0] = jnp.cos(ang)
    sin_ref[0] = jnp.sin(ang)


def _rope_tables(positions, half):
    nb, s = positions.shape
    out = jax.ShapeDtypeStruct((nb, s, half), F32)
    return pl.pallas_call(
        functools.partial(_rope_kernel, half=half),
        grid=(nb,),
        in_specs=[pl.BlockSpec((1, s, 1), lambda b: (b, 0, 0))],
        out_specs=[pl.BlockSpec((1, s, half), lambda b: (b, 0, 0))] * 2,
        out_shape=[out, out],
        compiler_params=_params("parallel"),
        name="rope_tables",
    )(positions.reshape(nb, s, 1))


def _ret_proj_kernel(h_ref, mod_ref, g_ref, w_ref, cos_ref, sin_ref, o_ref,
                     *, qk_cols, gate_col, k_scale):
    mod = mod_ref[0]
    hn = (_unit_rms(h_ref[...]) * g_ref[...] * (1.0 + mod[1:2, :]) + mod[0:1, :]).astype(BF16)
    cos = cos_ref[...]
    sin = sin_ref[...]
    half = cos.shape[-1]
    n = w_ref.shape[1]
    for c in range(0, 2 * qk_cols, 2 * half):
        acc = jnp.dot(hn, w_ref[:, c:c + 2 * half], preferred_element_type=F32)
        x1 = acc[:, :half]
        x2 = acc[:, half:]
        sc = k_scale if c >= qk_cols else 1.0
        o_ref[:, c:c + half] = ((x1 * cos - x2 * sin) * sc).astype(BF16)
        o_ref[:, c + half:c + 2 * half] = ((x1 * sin + x2 * cos) * sc).astype(BF16)
    step = 2 * MXU_DIM
    for c in range(2 * qk_cols, n, step):
        acc = jnp.dot(hn, w_ref[:, c:c + step], preferred_element_type=F32)
        o_ref[:, c:c + step] = (_silu(acc) if c >= gate_col else acc).astype(BF16)


def _ret_proj(h, mods, gain, w, cos, sin, seq, qk_cols, gate_col, k_scale):
    t, d = h.shape
    n = w.shape[1]
    tm = ROW_TILE
    half = cos.shape[-1]
    return pl.pallas_call(
        functools.partial(_ret_proj_kernel, qk_cols=qk_cols, gate_col=gate_col, k_scale=k_scale),
        grid=(t // tm,),
        in_specs=[
            pl.BlockSpec((tm, d), lambda i: (i, 0)),
            _per_seq(mods, seq // tm),
            _resident((1, d)),
            _resident((d, n)),
            pl.BlockSpec((tm, half), lambda i: (i, 0)),
            pl.BlockSpec((tm, half), lambda i: (i, 0)),
        ],
        out_specs=pl.BlockSpec((tm, n), lambda i: (i, 0)),
        out_shape=jax.ShapeDtypeStruct((t, n), BF16),
        compiler_params=_params("parallel"),
        name="ret_proj",
    )(h, mods, gain.reshape(1, d), w, cos, sin)


def _retention_kernel(q_ref, k_ref, v_ref, g_ref, o_ref, state_ref, *, chunk):
    n_chunks = q_ref.shape[1] // chunk
    head = jnp.full((1, 1), pl.program_id(1), jnp.int32).astype(F32)
    log_gamma = jnp.log(1.0 - jnp.exp2(-5.0 - head))
    row = lax.broadcasted_iota(jnp.int32, (chunk, chunk), 0).astype(F32)
    col = lax.broadcasted_iota(jnp.int32, (chunk, chunk), 1).astype(F32)
    rel = row - col
    intra = jnp.where(rel >= 0, jnp.exp(jnp.maximum(rel, 0.0) * log_gamma), 0.0)
    idx = row[:, 0:1]
    q_decay = jnp.exp((idx + 1.0) * log_gamma)
    k_decay = jnp.exp((chunk - 1.0 - idx) * log_gamma)
    chunk_decay = jnp.exp(chunk * log_gamma)

    for n in range(n_chunks):
        rows = slice(n * chunk, (n + 1) * chunk)
        qc = q_ref[0, rows, :]
        kc = k_ref[0, rows, :]
        vc = v_ref[0, rows, :]
        o = _mxu(_mxu_nt(qc, kc) * intra, vc)
        if n > 0:
            o = o + _mxu(qc.astype(F32) * q_decay, state_ref[...])
        if n < n_chunks - 1:
            update = _mxu_tn(kc.astype(F32) * k_decay, vc)
            state_ref[...] = update if n == 0 else state_ref[...] * chunk_decay + update
        o_ref[0, rows, :] = (g_ref[0, rows, :].astype(F32) * _unit_rms(o)).astype(BF16)


def _retention(proj, heads, dk, dv):
    nb, s, _ = proj.shape
    kb0 = heads
    vb0 = 2 * heads * dk // dv
    gb0 = vb0 + heads
    return pl.pallas_call(
        functools.partial(_retention_kernel, chunk=RET_BLOCK),
        grid=(nb, heads),
        in_specs=[
            pl.BlockSpec((1, s, dk), lambda b, h: (b, 0, h)),
            pl.BlockSpec((1, s, dk), lambda b, h: (b, 0, kb0 + h)),
            pl.BlockSpec((1, s, dv), lambda b, h: (b, 0, vb0 + h)),
            pl.BlockSpec((1, s, dv), lambda b, h: (b, 0, gb0 + h)),
        ],
        out_specs=pl.BlockSpec((1, s, dv), lambda b, h: (b, 0, h)),
        out_shape=jax.ShapeDtypeStruct((nb, s, heads * dv), BF16),
        scratch_shapes=[pltpu.VMEM((dk, dv), F32)],
        compiler_params=_params("parallel", "parallel"),
        name="retention",
    )(proj, proj, proj, proj)


def _out_proj_kernel(*refs, gate_row, norm_rows, sub_rows):
    a_ref, w_ref, h_ref, mod_ref = refs[:4]
    n_norm = len(norm_rows)
    norm_in = refs[4:4 + 2 * n_norm]
    o_ref = refs[4 + 2 * n_norm]
    hn_refs = refs[5 + 2 * n_norm:]
    gate = mod_ref[0][gate_row:gate_row + 1, :]
    for r0 in range(0, a_ref.shape[0], sub_rows):
        rows = slice(r0, r0 + sub_rows)
        h = h_ref[rows, :] + gate * jnp.dot(a_ref[rows, :], w_ref[...], preferred_element_type=F32)
        o_ref[rows, :] = h
        if n_norm:
            xn = _unit_rms(h)
            for i, (shift_row, scale_row) in enumerate(norm_rows):
                gain = norm_in[2 * i][...]
                mod = norm_in[2 * i + 1][0]
                hn_refs[i][rows, :] = (xn * gain * (1.0 + mod[scale_row:scale_row + 1, :])
                                       + mod[shift_row:shift_row + 1, :]).astype(BF16)


def _out_proj(a, w, h, mods, gate_row, seq, norms=()):
    t, k = a.shape
    d = w.shape[1]
    tm = OUT_ROW_TILE
    row_spec = pl.BlockSpec((tm, d), lambda i: (i, 0))
    in_specs = [pl.BlockSpec((tm, k), lambda i: (i, 0)), _resident((k, d)), row_spec,
                _per_seq(mods, seq // tm)]
    args = [a, w, h, mods]
    for gain, nmods, _, _ in norms:
        in_specs += [_resident((1, d)), _per_seq(nmods, seq // tm)]
        args += [gain.reshape(1, d), nmods]
    out = pl.pallas_call(
        functools.partial(_out_proj_kernel, gate_row=gate_row, sub_rows=MXU_DIM,
                          norm_rows=tuple((sh, sc) for _, _, sh, sc in norms)),
        grid=(t // tm,),
        in_specs=in_specs,
        out_specs=[row_spec] * (1 + len(norms)),
        out_shape=[jax.ShapeDtypeStruct((t, d), F32)]
        + [jax.ShapeDtypeStruct((t, d), BF16)] * len(norms),
        compiler_params=_params("parallel"),
        name="out_proj",
    )(*args)
    return out


def _ffn_up_kernel(hn_ref, w_ref, cw_ref, cb_ref, o_ref, carry_ref, buf_ref, *, per_seq, col_tile):
    tm = hn_ref.shape[0]
    f = o_ref.shape[1]
    halo = SUBLANES

    @pl.when(pl.program_id(0) % per_seq == 0)
    def _():
        carry_ref[...] = jnp.zeros_like(carry_ref)

    hn = hn_ref[...]

    def conv(c):
        u = jnp.dot(hn, w_ref[:, c:c + col_tile], preferred_element_type=F32)
        buf_ref[0:halo, :] = carry_ref[:, c:c + col_tile]
        buf_ref[halo:halo + tm, :] = u
        carry_ref[:, c:c + col_tile] = u[tm - halo:, :]
        y = cb_ref[:, c:c + col_tile] + buf_ref[halo - 2:halo - 2 + tm, :] * cw_ref[0:1, c:c + col_tile]
        y = y + buf_ref[halo - 1:halo - 1 + tm, :] * cw_ref[1:2, c:c + col_tile]
        return y + u * cw_ref[2:3, c:c + col_tile]

    for c in range(0, f, col_tile):
        val = conv(c)
        gate = conv(f + c)
        o_ref[:, c:c + col_tile] = (val * _silu(gate)).astype(BF16)


def _ffn_up(hn, w, conv_w, conv_b, seq):
    t, d = hn.shape
    f2 = w.shape[1]
    f = f2 // 2
    tm = ROW_TILE
    col_tile = MXU_DIM
    return pl.pallas_call(
        functools.partial(_ffn_up_kernel, per_seq=seq // tm, col_tile=col_tile),
        grid=(t // tm,),
        in_specs=[
            pl.BlockSpec((tm, d), lambda i: (i, 0)),
            _resident((d, f2)),
            _resident((CONV_WIDTH, f2)),
            _resident((1, f2)),
        ],
        out_specs=pl.BlockSpec((tm, f), lambda i: (i, 0)),
        out_shape=jax.ShapeDtypeStruct((t, f), BF16),
        scratch_shapes=[pltpu.VMEM((SUBLANES, f2), F32),
                        pltpu.VMEM((SUBLANES + tm, col_tile), F32)],
        compiler_params=_params("arbitrary"),
        name="ffn_up",
    )(hn, w, conv_w, conv_b.reshape(1, f2))


def _qkv_proj_kernel(hq_ref, hkv_ref, wq_ref, wkv_ref, qg_ref, kg_ref, q_ref, k_ref, v_ref,
                     *, head_dim, q_scale):
    tm, d = hq_ref.shape
    hq = hq_ref[...]
    hkv = hkv_ref[...]
    ct = MXU_DIM
    cols = range(0, d, ct)
    r = lax.broadcasted_iota(jnp.int32, (ct, ct), 0) // head_dim
    c = lax.broadcasted_iota(jnp.int32, (ct, ct), 1) // head_dim
    pool = jnp.where(r == c, 1.0 / head_dim, 0.0).astype(BF16)
    raw = [jnp.dot(hq, wq_ref[:, c0:c0 + ct], preferred_element_type=F32) for c0 in cols]
    raw += [jnp.dot(hkv, wkv_ref[:, c0:c0 + ct], preferred_element_type=F32) for c0 in cols]
    squares = jnp.concatenate([(x * x).astype(BF16) for x in raw], axis=0)
    inv_rms = lax.rsqrt(jnp.dot(squares, pool, preferred_element_type=F32) + EPS)
    for i, c0 in enumerate(cols):
        j = len(cols) + i
        q_ref[:, c0:c0 + ct] = (raw[i] * inv_rms[i * tm:(i + 1) * tm]
                                * (qg_ref[:, c0:c0 + ct] * q_scale)).astype(BF16)
        k_ref[:, c0:c0 + ct] = (raw[j] * inv_rms[j * tm:(j + 1) * tm]
                                * kg_ref[:, c0:c0 + ct]).astype(BF16)
    for c0 in cols:
        v_ref[:, c0:c0 + ct] = jnp.dot(hkv, wkv_ref[:, d + c0:d + c0 + ct],
                                       preferred_element_type=F32).astype(BF16)


def _qkv_proj(hq, hkv, wq, wkv, q_gain, k_gain, head_dim):
    t, d = hq.shape
    tm = ROW_TILE
    out = jax.ShapeDtypeStruct((t, d), BF16)
    row_spec = pl.BlockSpec((tm, d), lambda i: (i, 0))
    return pl.pallas_call(
        functools.partial(_qkv_proj_kernel, head_dim=head_dim,
                          q_scale=head_dim ** -0.5 * math.log2(math.e)),
        grid=(t // tm,),
        in_specs=[row_spec, row_spec, _resident((d, d)), _resident((d, 2 * d)),
                  _resident((1, d)), _resident((1, d))],
        out_specs=[row_spec, row_spec, row_spec],
        out_shape=[out, out, out],
        compiler_params=_params("parallel"),
        name="qkv_proj",
    )(hq, hkv, wq, wkv,
      jnp.tile(q_gain, d // head_dim).reshape(1, d), jnp.tile(k_gain, d // head_dim).reshape(1, d))


def _softplus2(z):
    return jnp.maximum(z, 0.0) + jnp.log2(1.0 + jnp.exp2(-jnp.abs(z)))


def _neg_suffix_sum(x, neg_upper):
    return jnp.dot(x, neg_upper, preferred_element_type=F32)


def _attn_kernel(q_ref, k_ref, v_ref, o_ref, *, tk, head_dim):
    assert tk & (tk - 1) == 0 and tk % LANES == 0
    tq = 2 * tk
    n_heads = LANES // head_dim
    half_rows = n_heads * tk
    n_blocks = q_ref.shape[1] // tq
    lane = lax.broadcasted_iota(jnp.int32, (1, LANES), 1)
    head_lanes = [(lane >= h * head_dim) & (lane < (h + 1) * head_dim) for h in range(n_heads)]
    row = lax.broadcasted_iota(jnp.int32, (tk, tk), 0)
    col = lax.broadcasted_iota(jnp.int32, (tk, tk), 1)
    neg_upper = jnp.where(row > col, -1.0, 0.0)
    row = lax.broadcasted_iota(jnp.int32, (half_rows, tk), 0)
    col = lax.broadcasted_iota(jnp.int32, (half_rows, tk), 1)
    diag_mask = col < (row & (tk - 1))

    def scores(qs, k0):
        return _mxu_nt(qs, k_ref[0, pl.ds(k0, tk), :])

    def consume(z, k0, mask, acc, later):
        sp = _softplus2(z)
        if mask is not None:
            sp = jnp.where(mask, sp, 0.0)
        after = _neg_suffix_sum(sp, neg_upper)
        p = jnp.exp2(z - sp + after)
        if mask is not None:
            p = jnp.where(mask, p, 0.0)
        pv = jnp.dot(p, v_ref[0, pl.ds(k0, tk), :].astype(F32), preferred_element_type=F32)
        acc = acc + jnp.exp2(later) * pv
        return acc, later + (after[:, 0:1] - sp[:, 0:1])

    def step(qs, k0, mask, acc, later):
        return consume(scores(qs, k0), k0, mask, acc, later)

    for qi in range(n_blocks):
        q0 = qi * tq
        q1 = q0 + tk
        q_lo = q_ref[0, q0:q0 + tk, :]
        q_hi = q_ref[0, q1:q1 + tk, :]
        q_hi = jnp.concatenate([jnp.where(m, q_hi, jnp.zeros_like(q_hi)) for m in head_lanes], axis=0)
        q_lo = jnp.concatenate([jnp.where(m, q_lo, jnp.zeros_like(q_lo)) for m in head_lanes], axis=0)
        q_all = jnp.concatenate([q_lo, q_hi], axis=0)
        zeros = jnp.zeros((half_rows, LANES), F32)
        acc_hi, later_hi = step(q_hi, q1, diag_mask, zeros, zeros)
        acc_hi, later_hi = step(q_hi, q0, None, acc_hi, later_hi)
        acc_lo, later_lo = step(q_lo, q0, diag_mask, zeros, zeros)
        acc = jnp.concatenate([acc_lo, acc_hi], axis=0)
        later = jnp.concatenate([later_lo, later_hi], axis=0)
        for kb in range(2 * qi - 1, -1, -1):
            acc, later = step(q_all, kb * tk, None, acc, later)
        for half, r0 in enumerate((q0, q1)):
            base = half * half_rows
            o = sum(jnp.where(m, acc[base + h * tk:base + (h + 1) * tk], 0.0)
                    for h, m in enumerate(head_lanes))
            o_ref[0, r0:r0 + tk, :] = o.astype(BF16)


def _attention(q, k, v, head_dim):
    nb, s, d = q.shape
    spec = pl.BlockSpec((1, s, LANES), lambda b, p: (b, 0, p))
    return pl.pallas_call(
        functools.partial(_attn_kernel, tk=ATTN_TILE, head_dim=head_dim),
        grid=(nb, d // LANES),
        in_specs=[spec, spec, spec],
        out_specs=spec,
        out_shape=jax.ShapeDtypeStruct((nb, s, d), BF16),
        compiler_params=_params("parallel", "parallel"),
        name="sb_attention",
    )(q, k, v)


def kernel(x, c, positions, ada_w, ada_b, norm_mix_g, norm_ffn_g, ret_w_in, ret_w_out, kv_ada_w, kv_ada_b, kv_norm_g, w_kv, k_norm_g, sb_w_q, q_norm_g, sb_w_out, ffn_w_in, ffn_conv_w, ffn_conv_b, ffn_w_out):
    nb, s, d = x.shape
    t = nb * s
    depth = ada_w.shape[0]
    n_mod = ada_w.shape[2] // d
    dk = d // RET_HEADS
    dv = 2 * dk
    head_dim = d // SB_HEADS
    shift_m, scale_m, gate_m, shift_f, scale_f, gate_f = range(6)

    mods = _ada(c, ada_w, ada_b).reshape(depth, nb, n_mod, d)
    kvmods = _ada(c, kv_ada_w[None], kv_ada_b[None]).reshape(nb, 2, d)
    cos, sin = _rope_tables(positions, dk // 2)
    cos = cos.reshape(t, dk // 2)
    sin = sin.reshape(t, dk // 2)

    h = x.reshape(t, d)

    proj = _ret_proj(h, mods[0], norm_mix_g[0], ret_w_in[0].astype(BF16), cos, sin,
                     s, RET_HEADS * dk, 2 * RET_HEADS * dk + RET_HEADS * dv, dk ** -0.5)
    og = _retention(proj.reshape(nb, s, -1), RET_HEADS, dk, dv)
    h, hf = _out_proj(og.reshape(t, -1), ret_w_out[0].astype(BF16), h, mods[0], gate_m, s,
                      norms=[(norm_ffn_g[0], mods[0], shift_f, scale_f)])
    act = _ffn_up(hf, ffn_w_in[0].astype(BF16), ffn_conv_w[0], ffn_conv_b[0], s)
    h, hq, hkv = _out_proj(act, ffn_w_out[0].astype(BF16), h, mods[0], gate_f, s,
                           norms=[(norm_mix_g[1], mods[1], shift_m, scale_m),
                                  (kv_norm_g, kvmods, 0, 1)])

    q, k, v = _qkv_proj(hq, hkv, sb_w_q[0].astype(BF16), w_kv.astype(BF16),
                        q_norm_g[0], k_norm_g, head_dim)
    o = _attention(q.reshape(nb, s, d), k.reshape(nb, s, d), v.reshape(nb, s, d), head_dim)
    h, hf = _out_proj(o.reshape(t, d), sb_w_out[0].astype(BF16), h, mods[1], gate_m, s,
                      norms=[(norm_ffn_g[1], mods[1], shift_f, scale_f)])
    act = _ffn_up(hf, ffn_w_in[1].astype(BF16), ffn_conv_w[1], ffn_conv_b[1], s)
    (h,) = _out_proj(act, ffn_w_out[1].astype(BF16), h, mods[1], gate_f, s)
    return h.reshape(nb, s, d)
```

```python
import functools
import math

import jax
import jax.numpy as jnp
from jax import lax
from jax.experimental import pallas as pl
from jax.experimental.pallas import tpu as pltpu

F32 = jnp.float32
BF16 = jnp.bfloat16

EPS = 1e-6
ROPE_BASE = 10000.0
RET_HEADS = 4
SB_HEADS = 16
CONV_WIDTH = 3

LANES = 128
SUBLANES = 8
MXU_DIM = 256
V7X_VMEM_BYTES = 64 * 1024 * 1024
VMEM_LIMIT_BYTES = (V7X_VMEM_BYTES * 7) // 8

ROW_TILE = 512
OUT_ROW_TILE = 1024
ATTN_TILE = 256
RET_BLOCK = MXU_DIM


def _params(*semantics):
    return pltpu.CompilerParams(dimension_semantics=semantics,
                                vmem_limit_bytes=VMEM_LIMIT_BYTES)


def _resident(shape, layer=None):
    zeros = (0,) * len(shape)
    if layer is None:
        return pl.BlockSpec(shape, lambda *_: zeros, pipeline_mode=pl.Buffered(1))
    return pl.BlockSpec((None,) + shape, lambda *_: (layer,) + zeros, pipeline_mode=pl.Buffered(1))


def _per_seq(mods, tiles_per_seq):
    return pl.BlockSpec((1,) + mods.shape[1:], lambda i: (i // tiles_per_seq, 0, 0))


def _silu(x):
    h = 0.5 * x
    return h + h * jnp.tanh(h)


def _unit_rms(x):
    return x * lax.rsqrt(jnp.mean(x * x, axis=-1, keepdims=True) + EPS)


def _mxu(a, b):
    return jnp.dot(a.astype(BF16), b.astype(BF16), preferred_element_type=F32)


def _mxu_nt(a, b):
    return lax.dot_general(a.astype(BF16), b.astype(BF16), (((1,), (1,)), ((), ())),
                           preferred_element_type=F32)


def _mxu_tn(a, b):
    return lax.dot_general(a.astype(BF16), b.astype(BF16), (((0,), (0,)), ((), ())),
                           preferred_element_type=F32)


def _ada_kernel(c_ref, w_ref, b_ref, o_ref):
    o_ref[...] = _mxu(_silu(c_ref[...]), w_ref[...]) + b_ref[...]


def _ada(c, w, b, col_tile=2048):
    nb, d = c.shape
    nl, _, n = w.shape
    return pl.pallas_call(
        _ada_kernel,
        grid=(nl, n // col_tile),
        in_specs=[
            pl.BlockSpec((nb, d), lambda l, j: (0, 0)),
            pl.BlockSpec((None, d, col_tile), lambda l, j: (l, 0, j)),
            pl.BlockSpec((None, 1, col_tile), lambda l, j: (l, 0, j)),
        ],
        out_specs=pl.BlockSpec((None, nb, col_tile), lambda l, j: (l, 0, j)),
        out_shape=jax.ShapeDtypeStruct((nl, nb, n), F32),
        compiler_params=_params("parallel", "parallel"),
        name="ada_mods",
    )(c, w, b.reshape(nl, 1, n))


def _rope_kernel(pos_ref, cos_ref, sin_ref, *, half):
    i = lax.broadcasted_iota(jnp.int32, (1, half), 1).astype(F32)
    inv = jnp.exp(i * (-math.log(ROPE_BASE) / half))
    pos = jnp.broadcast_to(pos_ref[0].astype(F32), (SUBLANES, pos_ref.shape[2]))
    ang = pos.T[:, 0:1] * inv
    cos_ref[0] = jnp.cos(ang)
    sin_ref[0] = jnp.sin(ang)


def _rope_tables(positions, half):
    nb, s = positions.shape
    out = jax.ShapeDtypeStruct((nb, s, half), F32)
    return pl.pallas_call(
        functools.partial(_rope_kernel, half=half),
        grid=(nb,),
        in_specs=[pl.BlockSpec((1, 1, s), lambda b: (b, 0, 0))],
        out_specs=[pl.BlockSpec((1, s, half), lambda b: (b, 0, 0))] * 2,
        out_shape=[out, out],
        compiler_params=_params("parallel"),
        name="rope_tables",
    )(positions.reshape(nb, 1, s))


def _ret_proj_kernel(h_ref, mod_ref, g_ref, w_ref, cos_ref, sin_ref, o_ref,
                     *, qk_cols, gate_col, k_scale):
    mod = mod_ref[0]
    hn = (_unit_rms(h_ref[...]) * g_ref[...] * (1.0 + mod[1:2, :]) + mod[0:1, :]).astype(BF16)
    cos = cos_ref[...]
    sin = sin_ref[...]
    half = cos.shape[-1]
    n = w_ref.shape[1]
    for c in range(0, 2 * qk_cols, 2 * half):
        acc = jnp.dot(hn, w_ref[:, c:c + 2 * half], preferred_element_type=F32)
        x1 = acc[:, :half]
        x2 = acc[:, half:]
        sc = k_scale if c >= qk_cols else 1.0
        o_ref[:, c:c + half] = ((x1 * cos - x2 * sin) * sc).astype(BF16)
        o_ref[:, c + half:c + 2 * half] = ((x1 * sin + x2 * cos) * sc).astype(BF16)
    step = 2 * MXU_DIM
    for c in range(2 * qk_cols, n, step):
        acc = jnp.dot(hn, w_ref[:, c:c + step], preferred_element_type=F32)
        o_ref[:, c:c + step] = (_silu(acc) if c >= gate_col else acc).astype(BF16)


def _ret_proj(h, mods, gain, w, cos, sin, seq, qk_cols, gate_col, k_scale):
    t, d = h.shape
    n = w.shape[2]
    tm = ROW_TILE
    half = cos.shape[-1]
    return pl.pallas_call(
        functools.partial(_ret_proj_kernel, qk_cols=qk_cols, gate_col=gate_col, k_scale=k_scale),
        grid=(t // tm,),
        in_specs=[
            pl.BlockSpec((tm, d), lambda i: (i, 0)),
            _per_seq(mods, seq // tm),
            _resident((1, d)),
            _resident((d, n), layer=0),
            pl.BlockSpec((tm, half), lambda i: (i, 0)),
            pl.BlockSpec((tm, half), lambda i: (i, 0)),
        ],
        out_specs=pl.BlockSpec((tm, n), lambda i: (i, 0)),
        out_shape=jax.ShapeDtypeStruct((t, n), BF16),
        compiler_params=_params("parallel"),
        name="ret_proj",
    )(h, mods, gain.reshape(1, d), w, cos, sin)


def _retention_kernel(q_ref, k_ref, v_ref, g_ref, o_ref, state_ref, *, chunk):
    n_chunks = q_ref.shape[1] // chunk
    head = jnp.full((1, 1), pl.program_id(1), jnp.int32).astype(F32)
    log_gamma = jnp.log(1.0 - jnp.exp2(-5.0 - head))
    row = lax.broadcasted_iota(jnp.int32, (chunk, chunk), 0).astype(F32)
    col = lax.broadcasted_iota(jnp.int32, (chunk, chunk), 1).astype(F32)
    rel = row - col
    intra = jnp.where(rel >= 0, jnp.exp(jnp.maximum(rel, 0.0) * log_gamma), 0.0)
    idx = row[:, 0:1]
    q_decay = jnp.exp((idx + 1.0) * log_gamma)
    k_decay = jnp.exp((chunk - 1.0 - idx) * log_gamma)
    chunk_decay = jnp.exp(chunk * log_gamma)

    for n in range(n_chunks):
        rows = slice(n * chunk, (n + 1) * chunk)
        qc = q_ref[0, rows, :]
        kc = k_ref[0, rows, :]
        vc = v_ref[0, rows, :]
        o = _mxu(_mxu_nt(qc, kc) * intra, vc)
        if n > 0:
            o = o + _mxu(qc.astype(F32) * q_decay, state_ref[...])
        if n < n_chunks - 1:
            update = _mxu_tn(kc.astype(F32) * k_decay, vc)
            state_ref[...] = update if n == 0 else state_ref[...] * chunk_decay + update
        o_ref[0, rows, :] = (g_ref[0, rows, :].astype(F32) * _unit_rms(o)).astype(BF16)


def _retention(proj, heads, dk, dv):
    nb, s, _ = proj.shape
    kb0 = heads
    vb0 = 2 * heads * dk // dv
    gb0 = vb0 + heads
    return pl.pallas_call(
        functools.partial(_retention_kernel, chunk=RET_BLOCK),
        grid=(nb, heads),
        in_specs=[
            pl.BlockSpec((1, s, dk), lambda b, h: (b, 0, h)),
            pl.BlockSpec((1, s, dk), lambda b, h: (b, 0, kb0 + h)),
            pl.BlockSpec((1, s, dv), lambda b, h: (b, 0, vb0 + h)),
            pl.BlockSpec((1, s, dv), lambda b, h: (b, 0, gb0 + h)),
        ],
        out_specs=pl.BlockSpec((1, s, dv), lambda b, h: (b, 0, h)),
        out_shape=jax.ShapeDtypeStruct((nb, s, heads * dv), BF16),
        scratch_shapes=[pltpu.VMEM((dk, dv), F32)],
        compiler_params=_params("parallel", "parallel"),
        name="retention",
    )(proj, proj, proj, proj)


def _out_proj_kernel(*refs, gate_row, norm_rows, sub_rows):
    a_ref, w_ref, h_ref, mod_ref = refs[:4]
    n_norm = len(norm_rows)
    norm_in = refs[4:4 + 2 * n_norm]
    o_ref = refs[4 + 2 * n_norm]
    hn_refs = refs[5 + 2 * n_norm:]
    gate = mod_ref[0][gate_row:gate_row + 1, :]
    for r0 in range(0, a_ref.shape[0], sub_rows):
        rows = slice(r0, r0 + sub_rows)
        h = h_ref[rows, :] + gate * jnp.dot(a_ref[rows, :], w_ref[...], preferred_element_type=F32)
        o_ref[rows, :] = h
        if n_norm:
            xn = _unit_rms(h)
            for i, (shift_row, scale_row) in enumerate(norm_rows):
                gain = norm_in[2 * i][...]
                mod = norm_in[2 * i + 1][0]
                hn_refs[i][rows, :] = (xn * gain * (1.0 + mod[scale_row:scale_row + 1, :])
                                       + mod[shift_row:shift_row + 1, :]).astype(BF16)


def _out_proj(a, w, layer, h, mods, gate_row, seq, norms=()):
    t, k = a.shape
    d = w.shape[2]
    tm = OUT_ROW_TILE
    row_spec = pl.BlockSpec((tm, d), lambda i: (i, 0))
    in_specs = [pl.BlockSpec((tm, k), lambda i: (i, 0)), _resident((k, d), layer), row_spec,
                _per_seq(mods, seq // tm)]
    args = [a, w, h, mods]
    for gain, nmods, _, _ in norms:
        in_specs += [_resident((1, d)), _per_seq(nmods, seq // tm)]
        args += [gain.reshape(1, d), nmods]
    out = pl.pallas_call(
        functools.partial(_out_proj_kernel, gate_row=gate_row, sub_rows=MXU_DIM,
                          norm_rows=tuple((sh, sc) for _, _, sh, sc in norms)),
        grid=(t // tm,),
        in_specs=in_specs,
        out_specs=[row_spec] * (1 + len(norms)),
        out_shape=[jax.ShapeDtypeStruct((t, d), F32)]
        + [jax.ShapeDtypeStruct((t, d), BF16)] * len(norms),
        compiler_params=_params("parallel"),
        name="out_proj",
    )(*args)
    return out


def _ffn_up_kernel(hn_ref, w_ref, cw_ref, cb_ref, o_ref, carry_ref, buf_ref, *, per_seq, col_tile):
    tm = hn_ref.shape[0]
    f = o_ref.shape[1]
    halo = SUBLANES

    @pl.when(pl.program_id(0) % per_seq == 0)
    def _():
        carry_ref[...] = jnp.zeros_like(carry_ref)

    hn = hn_ref[...]

    def conv(c):
        u = jnp.dot(hn, w_ref[:, c:c + col_tile], preferred_element_type=F32)
        buf_ref[0:halo, :] = carry_ref[:, c:c + col_tile]
        buf_ref[halo:halo + tm, :] = u
        carry_ref[:, c:c + col_tile] = u[tm - halo:, :]
        y = cb_ref[:, c:c + col_tile] + buf_ref[halo - 2:halo - 2 + tm, :] * cw_ref[0:1, c:c + col_tile]
        y = y + buf_ref[halo - 1:halo - 1 + tm, :] * cw_ref[1:2, c:c + col_tile]
        return y + u * cw_ref[2:3, c:c + col_tile]

    for c in range(0, f, col_tile):
        val = conv(c)
        gate = conv(f + c)
        o_ref[:, c:c + col_tile] = (val * _silu(gate)).astype(BF16)


def _ffn_up(hn, w, conv_w, conv_b, layer, seq):
    t, d = hn.shape
    f2 = w.shape[2]
    f = f2 // 2
    tm = ROW_TILE
    col_tile = MXU_DIM
    return pl.pallas_call(
        functools.partial(_ffn_up_kernel, per_seq=seq // tm, col_tile=col_tile),
        grid=(t // tm,),
        in_specs=[
            pl.BlockSpec((tm, d), lambda i: (i, 0)),
            _resident((d, f2), layer),
            _resident((CONV_WIDTH, f2), layer),
            _resident((1, f2), layer),
        ],
        out_specs=pl.BlockSpec((tm, f), lambda i: (i, 0)),
        out_shape=jax.ShapeDtypeStruct((t, f), BF16),
        scratch_shapes=[pltpu.VMEM((SUBLANES, f2), F32),
                        pltpu.VMEM((SUBLANES + tm, col_tile), F32)],
        compiler_params=_params("arbitrary"),
        name="ffn_up",
    )(hn, w, conv_w, conv_b.reshape(-1, 1, f2))


def _qkv_proj_kernel(hq_ref, hkv_ref, wq_ref, wkv_ref, qg_ref, kg_ref, q_ref, k_ref, v_ref,
                     *, head_dim, q_scale):
    tm, d = hq_ref.shape
    hq = hq_ref[...]
    hkv = hkv_ref[...]
    ct = MXU_DIM
    cols = range(0, d, ct)
    r = lax.broadcasted_iota(jnp.int32, (ct, ct), 0) // head_dim
    c = lax.broadcasted_iota(jnp.int32, (ct, ct), 1) // head_dim
    pool = jnp.where(r == c, 1.0 / head_dim, 0.0).astype(BF16)
    raw = [jnp.dot(hq, wq_ref[:, c0:c0 + ct], preferred_element_type=F32) for c0 in cols]
    raw += [jnp.dot(hkv, wkv_ref[:, c0:c0 + ct], preferred_element_type=F32) for c0 in cols]
    squares = jnp.concatenate([(x * x).astype(BF16) for x in raw], axis=0)
    inv_rms = lax.rsqrt(jnp.dot(squares, pool, preferred_element_type=F32) + EPS)
    for i, c0 in enumerate(cols):
        j = len(cols) + i
        q_ref[:, c0:c0 + ct] = (raw[i] * inv_rms[i * tm:(i + 1) * tm]
                                * (qg_ref[:, c0:c0 + ct] * q_scale)).astype(BF16)
        k_ref[:, c0:c0 + ct] = (raw[j] * inv_rms[j * tm:(j + 1) * tm]
                                * kg_ref[:, c0:c0 + ct]).astype(BF16)
    for c0 in cols:
        v_ref[:, c0:c0 + ct] = jnp.dot(hkv, wkv_ref[:, d + c0:d + c0 + ct],
                                       preferred_element_type=F32).astype(BF16)


def _qkv_proj(hq, hkv, wq, wkv, q_gain, k_gain, head_dim):
    t, d = hq.shape
    tm = OUT_ROW_TILE
    out = jax.ShapeDtypeStruct((t, d), BF16)
    row_spec = pl.BlockSpec((tm, d), lambda i: (i, 0))
    return pl.pallas_call(
        functools.partial(_qkv_proj_kernel, head_dim=head_dim,
                          q_scale=head_dim ** -0.5 * math.log2(math.e)),
        grid=(t // tm,),
        in_specs=[row_spec, row_spec, _resident((d, d), layer=0), _resident((d, 2 * d), layer=0),
                  _resident((1, d)), _resident((1, d))],
        out_specs=[row_spec, row_spec, row_spec],
        out_shape=[out, out, out],
        compiler_params=_params("parallel"),
        name="qkv_proj",
    )(hq, hkv, wq, wkv,
      jnp.tile(q_gain, d // head_dim).reshape(1, d), jnp.tile(k_gain, d // head_dim).reshape(1, d))


def _softplus2(z):
    return jnp.maximum(z, 0.0) + jnp.log2(1.0 + jnp.exp2(-jnp.abs(z)))


def _neg_suffix_sum(x, neg_upper):
    return jnp.dot(x, neg_upper, preferred_element_type=F32)


def _attn_kernel(q_ref, k_ref, v_ref, o_ref, *, tk, head_dim):
    assert tk & (tk - 1) == 0 and tk % LANES == 0
    tq = 2 * tk
    n_heads = LANES // head_dim
    half_rows = n_heads * tk
    n_blocks = q_ref.shape[1] // tq
    lane = lax.broadcasted_iota(jnp.int32, (1, LANES), 1)
    head_lanes = [(lane >= h * head_dim) & (lane < (h + 1) * head_dim) for h in range(n_heads)]
    row = lax.broadcasted_iota(jnp.int32, (tk, tk), 0)
    col = lax.broadcasted_iota(jnp.int32, (tk, tk), 1)
    neg_upper = jnp.where(row > col, -1.0, 0.0)
    row = lax.broadcasted_iota(jnp.int32, (half_rows, tk), 0)
    col = lax.broadcasted_iota(jnp.int32, (half_rows, tk), 1)
    diag_mask = col < (row & (tk - 1))

    def scores(qs, k0):
        return _mxu_nt(qs, k_ref[0, pl.ds(k0, tk), :])

    def consume(z, k0, mask, acc, later):
        sp = _softplus2(z)
        if mask is not None:
            sp = jnp.where(mask, sp, 0.0)
        after = _neg_suffix_sum(sp, neg_upper)
        p = jnp.exp2(z - sp + after)
        if mask is not None:
            p = jnp.where(mask, p, 0.0)
        pv = jnp.dot(p, v_ref[0, pl.ds(k0, tk), :].astype(F32), preferred_element_type=F32)
        acc = acc + jnp.exp2(later) * pv
        return acc, later + (after[:, 0:1] - sp[:, 0:1])

    def step(qs, k0, mask, acc, later):
        return consume(scores(qs, k0), k0, mask, acc, later)

    for qi in range(n_blocks):
        q0 = qi * tq
        q1 = q0 + tk
        q_lo = q_ref[0, q0:q0 + tk, :]
        q_hi = q_ref[0, q1:q1 + tk, :]
        q_hi = jnp.concatenate([jnp.where(m, q_hi, jnp.zeros_like(q_hi)) for m in head_lanes], axis=0)
        q_lo = jnp.concatenate([jnp.where(m, q_lo, jnp.zeros_like(q_lo)) for m in head_lanes], axis=0)
        q_all = jnp.concatenate([q_lo, q_hi], axis=0)
        zeros = jnp.zeros((half_rows, LANES), F32)
        acc_hi, later_hi = step(q_hi, q1, diag_mask, zeros, zeros)
        acc_hi, later_hi = step(q_hi, q0, None, acc_hi, later_hi)
        acc_lo, later_lo = step(q_lo, q0, diag_mask, zeros, zeros)
        acc = jnp.concatenate([acc_lo, acc_hi], axis=0)
        later = jnp.concatenate([later_lo, later_hi], axis=0)
        for kb in range(2 * qi - 1, -1, -1):
            acc, later = step(q_all, kb * tk, None, acc, later)
        for half, r0 in enumerate((q0, q1)):
            base = half * half_rows
            o = sum(jnp.where(m, acc[base + h * tk:base + (h + 1) * tk], 0.0)
                    for h, m in enumerate(head_lanes))
            o_ref[0, r0:r0 + tk, :] = o.astype(BF16)


def _attention(q, k, v, head_dim):
    nb, s, d = q.shape
    spec = pl.BlockSpec((1, s, LANES), lambda b, p: (b, 0, p))
    return pl.pallas_call(
        functools.partial(_attn_kernel, tk=ATTN_TILE, head_dim=head_dim),
        grid=(nb, d // LANES),
        in_specs=[spec, spec, spec],
        out_specs=spec,
        out_shape=jax.ShapeDtypeStruct((nb, s, d), BF16),
        compiler_params=_params("parallel", "parallel"),
        name="sb_attention",
    )(q, k, v)


def kernel(x, c, positions, ada_w, ada_b, norm_mix_g, norm_ffn_g, ret_w_in, ret_w_out, kv_ada_w, kv_ada_b, kv_norm_g, w_kv, k_norm_g, sb_w_q, q_norm_g, sb_w_out, ffn_w_in, ffn_conv_w, ffn_conv_b, ffn_w_out):
    nb, s, d = x.shape
    t = nb * s
    depth = ada_w.shape[0]
    n_mod = ada_w.shape[2] // d
    dk = d // RET_HEADS
    dv = 2 * dk
    head_dim = d // SB_HEADS
    shift_m, scale_m, gate_m, shift_f, scale_f, gate_f = range(6)

    mods = _ada(c, ada_w, ada_b).reshape(depth, nb, n_mod, d)
    kvmods = _ada(c, kv_ada_w[None], kv_ada_b[None]).reshape(nb, 2, d)
    cos, sin = _rope_tables(positions, dk // 2)
    cos = cos.reshape(t, dk // 2)
    sin = sin.reshape(t, dk // 2)

    h = x.reshape(t, d)

    ffn_w_in = ffn_w_in.astype(BF16)
    ffn_w_out = ffn_w_out.astype(BF16)

    proj = _ret_proj(h, mods[0], norm_mix_g[0], ret_w_in.astype(BF16), cos, sin,
                     s, RET_HEADS * dk, 2 * RET_HEADS * dk + RET_HEADS * dv, dk ** -0.5)
    og = _retention(proj.reshape(nb, s, -1), RET_HEADS, dk, dv)
    h, hf = _out_proj(og.reshape(t, -1), ret_w_out.astype(BF16), 0, h, mods[0], gate_m, s,
                      norms=[(norm_ffn_g[0], mods[0], shift_f, scale_f)])
    act = _ffn_up(hf, ffn_w_in, ffn_conv_w, ffn_conv_b, 0, s)
    h, hq, hkv = _out_proj(act, ffn_w_out, 0, h, mods[0], gate_f, s,
                           norms=[(norm_mix_g[1], mods[1], shift_m, scale_m),
                                  (kv_norm_g, kvmods, 0, 1)])

    q, k, v = _qkv_proj(hq, hkv, sb_w_q.astype(BF16), w_kv[None].astype(BF16),
                        q_norm_g[0], k_norm_g, head_dim)
    o = _attention(q.reshape(nb, s, d), k.reshape(nb, s, d), v.reshape(nb, s, d), head_dim)
    h, hf = _out_proj(o.reshape(t, d), sb_w_out.astype(BF16), 0, h, mods[1], gate_m, s,
                      norms=[(norm_ffn_g[1], mods[1], shift_f, scale_f)])
    act = _ffn_up(hf, ffn_w_in, ffn_conv_w, ffn_conv_b, 1, s)
    (h,) = _out_proj(act, ffn_w_out, 1, h, mods[1], gate_f, s)
    return h.reshape(nb, s, d)
```

```python
import functools
import math

import jax
import jax.numpy as jnp
from jax import lax
from jax.experimental import pallas as pl
from jax.experimental.pallas import tpu as pltpu

F32 = jnp.float32
BF16 = jnp.bfloat16

EPS = 1e-6
ROPE_BASE = 10000.0
RET_HEADS = 4
SB_HEADS = 16
CONV_WIDTH = 3

LANES = 128
SUBLANES = 8
MXU_DIM = 256
V7X_VMEM_BYTES = 64 * 1024 * 1024
VMEM_LIMIT_BYTES = (V7X_VMEM_BYTES * 7) // 8

ROW_TILE = 512
OUT_ROW_TILE = 1024
ATTN_TILE = 256
RET_BLOCK = MXU_DIM


def _params(*semantics):
    return pltpu.CompilerParams(dimension_semantics=semantics,
                                vmem_limit_bytes=VMEM_LIMIT_BYTES)


def _resident(shape, layer=None):
    zeros = (0,) * len(shape)
    if layer is None:
        return pl.BlockSpec(shape, lambda *_: zeros, pipeline_mode=pl.Buffered(1))
    return pl.BlockSpec((None,) + shape, lambda *_: (layer,) + zeros, pipeline_mode=pl.Buffered(1))


def _per_seq(mods, tiles_per_seq):
    return pl.BlockSpec((1,) + mods.shape[1:], lambda i: (i // tiles_per_seq, 0, 0))


def _silu(x):
    h = 0.5 * x
    return h + h * jnp.tanh(h)


def _unit_rms(x):
    return x * lax.rsqrt(jnp.mean(x * x, axis=-1, keepdims=True) + EPS)


def _mxu(a, b):
    return jnp.dot(a.astype(BF16), b.astype(BF16), preferred_element_type=F32)


def _mxu_nt(a, b):
    return lax.dot_general(a.astype(BF16), b.astype(BF16), (((1,), (1,)), ((), ())),
                           preferred_element_type=F32)


def _mxu_tn(a, b):
    return lax.dot_general(a.astype(BF16), b.astype(BF16), (((0,), (0,)), ((), ())),
                           preferred_element_type=F32)


def _ada_kernel(c_ref, w_ref, b_ref, o_ref):
    o_ref[...] = _mxu(_silu(c_ref[...]), w_ref[...]) + b_ref[...]


def _ada(c, w, b, col_tile=2048):
    nb, d = c.shape
    nl, _, n = w.shape
    return pl.pallas_call(
        _ada_kernel,
        grid=(nl, n // col_tile),
        in_specs=[
            pl.BlockSpec((nb, d), lambda l, j: (0, 0)),
            pl.BlockSpec((None, d, col_tile), lambda l, j: (l, 0, j)),
            pl.BlockSpec((None, 1, col_tile), lambda l, j: (l, 0, j)),
        ],
        out_specs=pl.BlockSpec((None, nb, col_tile), lambda l, j: (l, 0, j)),
        out_shape=jax.ShapeDtypeStruct((nl, nb, n), F32),
        compiler_params=_params("parallel", "parallel"),
        name="ada_mods",
    )(c, w, b.reshape(nl, 1, n))


def _rope_kernel(pos_ref, cos_ref, sin_ref, *, half):
    i = lax.broadcasted_iota(jnp.int32, (1, half), 1).astype(F32)
    inv = jnp.exp(i * (-math.log(ROPE_BASE) / half))
    pos = jnp.broadcast_to(pos_ref[0].astype(F32), (SUBLANES, pos_ref.shape[2]))
    ang = pos.T[:, 0:1] * inv
    cos_ref[0] = jnp.cos(ang)
    sin_ref[0] = jnp.sin(ang)


def _rope_tables(positions, half):
    nb, s = positions.shape
    out = jax.ShapeDtypeStruct((nb, s, half), F32)
    return pl.pallas_call(
        functools.partial(_rope_kernel, half=half),
        grid=(nb,),
        in_specs=[pl.BlockSpec((1, 1, s), lambda b: (b, 0, 0))],
        out_specs=[pl.BlockSpec((1, s, half), lambda b: (b, 0, 0))] * 2,
        out_shape=[out, out],
        compiler_params=_params("parallel"),
        name="rope_tables",
    )(positions.reshape(nb, 1, s))


def _ret_proj_kernel(h_ref, mod_ref, g_ref, w_ref, cos_ref, sin_ref, o_ref,
                     *, qk_cols, gate_col, k_scale):
    mod = mod_ref[0]
    hn = (_unit_rms(h_ref[...]) * g_ref[...] * (1.0 + mod[1:2, :]) + mod[0:1, :]).astype(BF16)
    cos = cos_ref[...]
    sin = sin_ref[...]
    half = cos.shape[-1]
    n = w_ref.shape[1]
    for c in range(0, 2 * qk_cols, 2 * half):
        acc = jnp.dot(hn, w_ref[:, c:c + 2 * half], preferred_element_type=F32)
        x1 = acc[:, :half]
        x2 = acc[:, half:]
        sc = k_scale if c >= qk_cols else 1.0
        o_ref[:, c:c + half] = ((x1 * cos - x2 * sin) * sc).astype(BF16)
        o_ref[:, c + half:c + 2 * half] = ((x1 * sin + x2 * cos) * sc).astype(BF16)
    step = 2 * MXU_DIM
    for c in range(2 * qk_cols, n, step):
        acc = jnp.dot(hn, w_ref[:, c:c + step], preferred_element_type=F32)
        o_ref[:, c:c + step] = (_silu(acc) if c >= gate_col else acc).astype(BF16)


def _ret_proj(h, mods, gain, w, cos, sin, seq, qk_cols, gate_col, k_scale):
    t, d = h.shape
    n = w.shape[2]
    tm = ROW_TILE
    half = cos.shape[-1]
    return pl.pallas_call(
        functools.partial(_ret_proj_kernel, qk_cols=qk_cols, gate_col=gate_col, k_scale=k_scale),
        grid=(t // tm,),
        in_specs=[
            pl.BlockSpec((tm, d), lambda i: (i, 0)),
            _per_seq(mods, seq // tm),
            _resident((1, d)),
            _resident((d, n), layer=0),
            pl.BlockSpec((tm, half), lambda i: (i, 0)),
            pl.BlockSpec((tm, half), lambda i: (i, 0)),
        ],
        out_specs=pl.BlockSpec((tm, n), lambda i: (i, 0)),
        out_shape=jax.ShapeDtypeStruct((t, n), BF16),
        compiler_params=_params("parallel"),
        name="ret_proj",
    )(h, mods, gain.reshape(1, d), w, cos, sin)


def _retention_kernel(q_ref, k_ref, v_ref, g_ref, o_ref, state_ref, *, chunk):
    n_chunks = q_ref.shape[1] // chunk
    head = jnp.full((1, 1), pl.program_id(1), jnp.int32).astype(F32)
    log_gamma = jnp.log(1.0 - jnp.exp2(-5.0 - head))
    row = lax.broadcasted_iota(jnp.int32, (chunk, chunk), 0).astype(F32)
    col = lax.broadcasted_iota(jnp.int32, (chunk, chunk), 1).astype(F32)
    rel = row - col
    intra = jnp.where(rel >= 0, jnp.exp(jnp.maximum(rel, 0.0) * log_gamma), 0.0)
    idx = row[:, 0:1]
    q_decay = jnp.exp((idx + 1.0) * log_gamma)
    k_decay = jnp.exp((chunk - 1.0 - idx) * log_gamma)
    chunk_decay = jnp.exp(chunk * log_gamma)

    for n in range(n_chunks):
        rows = slice(n * chunk, (n + 1) * chunk)
        qc = q_ref[0, rows, :]
        kc = k_ref[0, rows, :]
        vc = v_ref[0, rows, :]
        o = _mxu(_mxu_nt(qc, kc) * intra, vc)
        if n > 0:
            o = o + _mxu(qc.astype(F32) * q_decay, state_ref[...])
        if n < n_chunks - 1:
            update = _mxu_tn(kc.astype(F32) * k_decay, vc)
            state_ref[...] = update if n == 0 else state_ref[...] * chunk_decay + update
        o_ref[0, rows, :] = (g_ref[0, rows, :].astype(F32) * _unit_rms(o)).astype(BF16)


def _retention(proj, heads, dk, dv):
    nb, s, _ = proj.shape
    kb0 = heads
    vb0 = 2 * heads * dk // dv
    gb0 = vb0 + heads
    return pl.pallas_call(
        functools.partial(_retention_kernel, chunk=RET_BLOCK),
        grid=(nb, heads),
        in_specs=[
            pl.BlockSpec((1, s, dk), lambda b, h: (b, 0, h)),
            pl.BlockSpec((1, s, dk), lambda b, h: (b, 0, kb0 + h)),
            pl.BlockSpec((1, s, dv), lambda b, h: (b, 0, vb0 + h)),
            pl.BlockSpec((1, s, dv), lambda b, h: (b, 0, gb0 + h)),
        ],
        out_specs=pl.BlockSpec((1, s, dv), lambda b, h: (b, 0, h)),
        out_shape=jax.ShapeDtypeStruct((nb, s, heads * dv), BF16),
        scratch_shapes=[pltpu.VMEM((dk, dv), F32)],
        compiler_params=_params("parallel", "parallel"),
        name="retention",
    )(proj, proj, proj, proj)


def _out_proj_kernel(*refs, gate_row, norm_rows, sub_rows):
    a_ref, w_ref, h_ref, mod_ref = refs[:4]
    n_norm = len(norm_rows)
    norm_in = refs[4:4 + 2 * n_norm]
    o_ref = refs[4 + 2 * n_norm]
    hn_refs = refs[5 + 2 * n_norm:]
    gate = mod_ref[0][gate_row:gate_row + 1, :]
    for r0 in range(0, a_ref.shape[0], sub_rows):
        rows = slice(r0, r0 + sub_rows)
        h = h_ref[rows, :] + gate * jnp.dot(a_ref[rows, :], w_ref[...], preferred_element_type=F32)
        o_ref[rows, :] = h
        if n_norm:
            xn = _unit_rms(h)
            for i, (shift_row, scale_row) in enumerate(norm_rows):
                gain = norm_in[2 * i][...]
                mod = norm_in[2 * i + 1][0]
                hn_refs[i][rows, :] = (xn * gain * (1.0 + mod[scale_row:scale_row + 1, :])
                                       + mod[shift_row:shift_row + 1, :]).astype(BF16)


def _out_proj(a, w, layer, h, mods, gate_row, seq, norms=()):
    t, k = a.shape
    d = w.shape[2]
    tm = OUT_ROW_TILE
    row_spec = pl.BlockSpec((tm, d), lambda i: (i, 0))
    in_specs = [pl.BlockSpec((tm, k), lambda i: (i, 0)), _resident((k, d), layer), row_spec,
                _per_seq(mods, seq // tm)]
    args = [a, w, h, mods]
    for gain, nmods, _, _ in norms:
        in_specs += [_resident((1, d)), _per_seq(nmods, seq // tm)]
        args += [gain.reshape(1, d), nmods]
    out = pl.pallas_call(
        functools.partial(_out_proj_kernel, gate_row=gate_row, sub_rows=MXU_DIM,
                          norm_rows=tuple((sh, sc) for _, _, sh, sc in norms)),
        grid=(t // tm,),
        in_specs=in_specs,
        out_specs=[row_spec] * (1 + len(norms)),
        out_shape=[jax.ShapeDtypeStruct((t, d), F32)]
        + [jax.ShapeDtypeStruct((t, d), BF16)] * len(norms),
        compiler_params=_params("parallel"),
        name="out_proj",
    )(*args)
    return out


def _ffn_up_kernel(hn_ref, w_ref, cw_ref, cb_ref, o_ref, carry_ref, buf_ref, *, per_seq, col_tile):
    tm = hn_ref.shape[0]
    f = o_ref.shape[1]
    halo = SUBLANES

    @pl.when(pl.program_id(0) % per_seq == 0)
    def _():
        carry_ref[...] = jnp.zeros_like(carry_ref)

    hn = hn_ref[...]

    def conv(c, scale):
        u = jnp.dot(hn, w_ref[:, c:c + col_tile], preferred_element_type=F32)
        buf_ref[0:halo, :] = carry_ref[:, c:c + col_tile]
        buf_ref[halo:halo + tm, :] = u
        carry_ref[:, c:c + col_tile] = u[tm - halo:, :]
        taps = cw_ref[:, c:c + col_tile] * scale
        y = cb_ref[:, c:c + col_tile] * scale + buf_ref[halo - 2:halo - 2 + tm, :] * taps[0:1]
        y = y + buf_ref[halo - 1:halo - 1 + tm, :] * taps[1:2]
        return y + u * taps[2:3]

    for c in range(0, f, col_tile):
        val = conv(c, 1.0)
        half_gate = conv(f + c, 0.5)
        silu_gate = half_gate + half_gate * jnp.tanh(half_gate)
        o_ref[:, c:c + col_tile] = (val * silu_gate).astype(BF16)


def _ffn_up(hn, w, conv_w, conv_b, layer, seq):
    t, d = hn.shape
    f2 = w.shape[2]
    f = f2 // 2
    tm = ROW_TILE
    col_tile = MXU_DIM
    return pl.pallas_call(
        functools.partial(_ffn_up_kernel, per_seq=seq // tm, col_tile=col_tile),
        grid=(t // tm,),
        in_specs=[
            pl.BlockSpec((tm, d), lambda i: (i, 0)),
            _resident((d, f2), layer),
            _resident((CONV_WIDTH, f2), layer),
            _resident((1, f2), layer),
        ],
        out_specs=pl.BlockSpec((tm, f), lambda i: (i, 0)),
        out_shape=jax.ShapeDtypeStruct((t, f), BF16),
        scratch_shapes=[pltpu.VMEM((SUBLANES, f2), F32),
                        pltpu.VMEM((SUBLANES + tm, col_tile), F32)],
        compiler_params=_params("arbitrary"),
        name="ffn_up",
    )(hn, w, conv_w, conv_b.reshape(-1, 1, f2))


def _qkv_proj_kernel(hq_ref, hkv_ref, wq_ref, wkv_ref, qg_ref, kg_ref, q_ref, k_ref, v_ref,
                     *, head_dim, q_scale):
    tm, d = hq_ref.shape
    hq = hq_ref[...]
    hkv = hkv_ref[...]
    ct = MXU_DIM
    cols = range(0, d, ct)
    r = lax.broadcasted_iota(jnp.int32, (ct, ct), 0) // head_dim
    c = lax.broadcasted_iota(jnp.int32, (ct, ct), 1) // head_dim
    pool = jnp.where(r == c, 1.0 / head_dim, 0.0).astype(BF16)
    raw = [jnp.dot(hq, wq_ref[:, c0:c0 + ct], preferred_element_type=F32) for c0 in cols]
    raw += [jnp.dot(hkv, wkv_ref[:, c0:c0 + ct], preferred_element_type=F32) for c0 in cols]
    squares = jnp.concatenate([(x * x).astype(BF16) for x in raw], axis=0)
    inv_rms = lax.rsqrt(jnp.dot(squares, pool, preferred_element_type=F32) + EPS)
    for i, c0 in enumerate(cols):
        j = len(cols) + i
        q_ref[:, c0:c0 + ct] = (raw[i] * inv_rms[i * tm:(i + 1) * tm]
                                * (qg_ref[:, c0:c0 + ct] * q_scale)).astype(BF16)
        k_ref[:, c0:c0 + ct] = (raw[j] * inv_rms[j * tm:(j + 1) * tm]
                                * kg_ref[:, c0:c0 + ct]).astype(BF16)
    for c0 in cols:
        v_ref[:, c0:c0 + ct] = jnp.dot(hkv, wkv_ref[:, d + c0:d + c0 + ct],
                                       preferred_element_type=F32).astype(BF16)


def _qkv_proj(hq, hkv, wq, wkv, q_gain, k_gain, head_dim):
    t, d = hq.shape
    tm = OUT_ROW_TILE
    out = jax.ShapeDtypeStruct((t, d), BF16)
    row_spec = pl.BlockSpec((tm, d), lambda i: (i, 0))
    return pl.pallas_call(
        functools.partial(_qkv_proj_kernel, head_dim=head_dim,
                          q_scale=head_dim ** -0.5 * math.log2(math.e)),
        grid=(t // tm,),
        in_specs=[row_spec, row_spec, _resident((d, d), layer=0), _resident((d, 2 * d), layer=0),
                  _resident((1, d)), _resident((1, d))],
        out_specs=[row_spec, row_spec, row_spec],
        out_shape=[out, out, out],
        compiler_params=_params("parallel"),
        name="qkv_proj",
    )(hq, hkv, wq, wkv,
      jnp.tile(q_gain, d // head_dim).reshape(1, d), jnp.tile(k_gain, d // head_dim).reshape(1, d))


def _softplus2(z):
    return jnp.maximum(z, 0.0) + jnp.log2(1.0 + jnp.exp2(-jnp.abs(z)))


def _neg_suffix_sum(x, neg_upper):
    return jnp.dot(x, neg_upper, preferred_element_type=F32)


def _attn_kernel(q_ref, k_ref, v_ref, o_ref, *, tk, head_dim):
    assert tk & (tk - 1) == 0 and tk % LANES == 0
    tq = 2 * tk
    n_heads = LANES // head_dim
    half_rows = n_heads * tk
    n_blocks = q_ref.shape[1] // tq
    lane = lax.broadcasted_iota(jnp.int32, (1, LANES), 1)
    head_lanes = [(lane >= h * head_dim) & (lane < (h + 1) * head_dim) for h in range(n_heads)]
    row = lax.broadcasted_iota(jnp.int32, (tk, tk), 0)
    col = lax.broadcasted_iota(jnp.int32, (tk, tk), 1)
    neg_upper = jnp.where(row > col, -1.0, 0.0)
    row = lax.broadcasted_iota(jnp.int32, (half_rows, tk), 0)
    col = lax.broadcasted_iota(jnp.int32, (half_rows, tk), 1)
    diag_mask = col < (row & (tk - 1))

    def scores(qs, k0):
        return _mxu_nt(qs, k_ref[0, pl.ds(k0, tk), :])

    def consume(z, k0, mask, acc, later):
        sp = _softplus2(z)
        if mask is not None:
            sp = jnp.where(mask, sp, 0.0)
        after = _neg_suffix_sum(sp, neg_upper)
        p = jnp.exp2(z - sp + after)
        if mask is not None:
            p = jnp.where(mask, p, 0.0)
        pv = jnp.dot(p, v_ref[0, pl.ds(k0, tk), :].astype(F32), preferred_element_type=F32)
        acc = acc + jnp.exp2(later) * pv
        return acc, later + (after[:, 0:1] - sp[:, 0:1])

    def step(qs, k0, mask, acc, later):
        return consume(scores(qs, k0), k0, mask, acc, later)

    for qi in range(n_blocks):
        q0 = qi * tq
        q1 = q0 + tk
        q_lo = q_ref[0, q0:q0 + tk, :]
        q_hi = q_ref[0, q1:q1 + tk, :]
        q_hi = jnp.concatenate([jnp.where(m, q_hi, jnp.zeros_like(q_hi)) for m in head_lanes], axis=0)
        q_lo = jnp.concatenate([jnp.where(m, q_lo, jnp.zeros_like(q_lo)) for m in head_lanes], axis=0)
        q_all = jnp.concatenate([q_lo, q_hi], axis=0)
        zeros = jnp.zeros((half_rows, LANES), F32)
        acc_hi, later_hi = step(q_hi, q1, diag_mask, zeros, zeros)
        acc_hi, later_hi = step(q_hi, q0, None, acc_hi, later_hi)
        acc_lo, later_lo = step(q_lo, q0, diag_mask, zeros, zeros)
        acc = jnp.concatenate([acc_lo, acc_hi], axis=0)
        later = jnp.concatenate([later_lo, later_hi], axis=0)
        for kb in range(2 * qi - 1, -1, -1):
            acc, later = step(q_all, kb * tk, None, acc, later)
        for half, r0 in enumerate((q0, q1)):
            base = half * half_rows
            o = sum(jnp.where(m, acc[base + h * tk:base + (h + 1) * tk], 0.0)
                    for h, m in enumerate(head_lanes))
            o_ref[0, r0:r0 + tk, :] = o.astype(BF16)


def _attention(q, k, v, head_dim):
    nb, s, d = q.shape
    spec = pl.BlockSpec((1, s, LANES), lambda b, p: (b, 0, p))
    return pl.pallas_call(
        functools.partial(_attn_kernel, tk=ATTN_TILE, head_dim=head_dim),
        grid=(nb, d // LANES),
        in_specs=[spec, spec, spec],
        out_specs=spec,
        out_shape=jax.ShapeDtypeStruct((nb, s, d), BF16),
        compiler_params=_params("parallel", "parallel"),
        name="sb_attention",
    )(q, k, v)


def kernel(x, c, positions, ada_w, ada_b, norm_mix_g, norm_ffn_g, ret_w_in, ret_w_out, kv_ada_w, kv_ada_b, kv_norm_g, w_kv, k_norm_g, sb_w_q, q_norm_g, sb_w_out, ffn_w_in, ffn_conv_w, ffn_conv_b, ffn_w_out):
    nb, s, d = x.shape
    t = nb * s
    depth = ada_w.shape[0]
    n_mod = ada_w.shape[2] // d
    dk = d // RET_HEADS
    dv = 2 * dk
    head_dim = d // SB_HEADS
    shift_m, scale_m, gate_m, shift_f, scale_f, gate_f = range(6)

    mods = _ada(c, ada_w, ada_b).reshape(depth, nb, n_mod, d)
    kvmods = _ada(c, kv_ada_w[None], kv_ada_b[None]).reshape(nb, 2, d)
    cos, sin = _rope_tables(positions, dk // 2)
    cos = cos.reshape(t, dk // 2)
    sin = sin.reshape(t, dk // 2)

    h = x.reshape(t, d)

    ffn_w_in = ffn_w_in.astype(BF16)
    ffn_w_out = ffn_w_out.astype(BF16)

    proj = _ret_proj(h, mods[0], norm_mix_g[0], ret_w_in.astype(BF16), cos, sin,
                     s, RET_HEADS * dk, 2 * RET_HEADS * dk + RET_HEADS * dv, dk ** -0.5)
    og = _retention(proj.reshape(nb, s, -1), RET_HEADS, dk, dv)
    h, hf = _out_proj(og.reshape(t, -1), ret_w_out.astype(BF16), 0, h, mods[0], gate_m, s,
                      norms=[(norm_ffn_g[0], mods[0], shift_f, scale_f)])
    act = _ffn_up(hf, ffn_w_in, ffn_conv_w, ffn_conv_b, 0, s)
    h, hq, hkv = _out_proj(act, ffn_w_out, 0, h, mods[0], gate_f, s,
                           norms=[(norm_mix_g[1], mods[1], shift_m, scale_m),
                                  (kv_norm_g, kvmods, 0, 1)])

    q, k, v = _qkv_proj(hq, hkv, sb_w_q.astype(BF16), w_kv[None].astype(BF16),
                        q_norm_g[0], k_norm_g, head_dim)
    o = _attention(q.reshape(nb, s, d), k.reshape(nb, s, d), v.reshape(nb, s, d), head_dim)
    h, hf = _out_proj(o.reshape(t, d), sb_w_out.astype(BF16), 0, h, mods[1], gate_m, s,
                      norms=[(norm_ffn_g[1], mods[1], shift_f, scale_f)])
    act = _ffn_up(hf, ffn_w_in, ffn_conv_w, ffn_conv_b, 1, s)
    (h,) = _out_proj(act, ffn_w_out, 1, h, mods[1], gate_f, s)
    return h.reshape(nb, s, d)
```

```python
import functools
import math

import jax
import jax.numpy as jnp
from jax import lax
from jax.experimental import pallas as pl
from jax.experimental.pallas import tpu as pltpu

F32 = jnp.float32
BF16 = jnp.bfloat16

EPS = 1e-6
ROPE_BASE = 10000.0
RET_HEADS = 4
SB_HEADS = 16
CONV_WIDTH = 3

LANES = 128
SUBLANES = 8
MXU_DIM = 256
V7X_VMEM_BYTES = 64 * 1024 * 1024
VMEM_LIMIT_BYTES = (V7X_VMEM_BYTES * 7) // 8

ROW_TILE = 512
OUT_ROW_TILE = 1024
ATTN_TILE = 256
RET_BLOCK = MXU_DIM


def _params(*semantics):
    return pltpu.CompilerParams(dimension_semantics=semantics,
                                vmem_limit_bytes=VMEM_LIMIT_BYTES)


def _resident(shape, layer=None):
    zeros = (0,) * len(shape)
    if layer is None:
        return pl.BlockSpec(shape, lambda *_: zeros, pipeline_mode=pl.Buffered(1))
    return pl.BlockSpec((None,) + shape, lambda *_: (layer,) + zeros, pipeline_mode=pl.Buffered(1))


def _per_seq(mods, tiles_per_seq):
    return pl.BlockSpec((1,) + mods.shape[1:], lambda i: (i // tiles_per_seq, 0, 0))


def _silu(x):
    h = 0.5 * x
    return h + h * jnp.tanh(h)


def _unit_rms(x):
    return x * lax.rsqrt(jnp.mean(x * x, axis=-1, keepdims=True) + EPS)


def _mxu(a, b):
    return jnp.dot(a.astype(BF16), b.astype(BF16), preferred_element_type=F32)


def _mxu_nt(a, b):
    return lax.dot_general(a.astype(BF16), b.astype(BF16), (((1,), (1,)), ((), ())),
                           preferred_element_type=F32)


def _mxu_tn(a, b):
    return lax.dot_general(a.astype(BF16), b.astype(BF16), (((0,), (0,)), ((), ())),
                           preferred_element_type=F32)


def _ada_kernel(c_ref, w_ref, b_ref, o_ref):
    o_ref[...] = _mxu(_silu(c_ref[...]), w_ref[...]) + b_ref[...]


def _ada(c, w, b, col_tile=2048):
    nb, d = c.shape
    nl, _, n = w.shape
    return pl.pallas_call(
        _ada_kernel,
        grid=(nl, n // col_tile),
        in_specs=[
            pl.BlockSpec((nb, d), lambda l, j: (0, 0)),
            pl.BlockSpec((None, d, col_tile), lambda l, j: (l, 0, j)),
            pl.BlockSpec((None, 1, col_tile), lambda l, j: (l, 0, j)),
        ],
        out_specs=pl.BlockSpec((None, nb, col_tile), lambda l, j: (l, 0, j)),
        out_shape=jax.ShapeDtypeStruct((nl, nb, n), F32),
        compiler_params=_params("parallel", "parallel"),
        name="ada_mods",
    )(c, w, b.reshape(nl, 1, n))


def _rope_kernel(pos_ref, cos_ref, sin_ref, *, half):
    i = lax.broadcasted_iota(jnp.int32, (1, half), 1).astype(F32)
    inv = jnp.exp(i * (-math.log(ROPE_BASE) / half))
    pos = jnp.broadcast_to(pos_ref[0].astype(F32), (SUBLANES, pos_ref.shape[2]))
    ang = pos.T[:, 0:1] * inv
    cos_ref[0] = jnp.cos(ang)
    sin_ref[0] = jnp.sin(ang)


def _rope_tables(positions, half):
    nb, s = positions.shape
    out = jax.ShapeDtypeStruct((nb, s, half), F32)
    return pl.pallas_call(
        functools.partial(_rope_kernel, half=half),
        grid=(nb,),
        in_specs=[pl.BlockSpec((1, 1, s), lambda b: (b, 0, 0))],
        out_specs=[pl.BlockSpec((1, s, half), lambda b: (b, 0, 0))] * 2,
        out_shape=[out, out],
        compiler_params=_params("parallel"),
        name="rope_tables",
    )(positions.reshape(nb, 1, s))


def _ret_proj_kernel(h_ref, mod_ref, g_ref, w_ref, cos_ref, sin_ref, o_ref,
                     *, qk_cols, gate_col, k_scale):
    mod = mod_ref[0]
    hn = (_unit_rms(h_ref[...]) * g_ref[...] * (1.0 + mod[1:2, :]) + mod[0:1, :]).astype(BF16)
    cos = cos_ref[...]
    sin = sin_ref[...]
    half = cos.shape[-1]
    n = w_ref.shape[1]
    for c in range(0, 2 * qk_cols, 2 * half):
        acc = jnp.dot(hn, w_ref[:, c:c + 2 * half], preferred_element_type=F32)
        x1 = acc[:, :half]
        x2 = acc[:, half:]
        sc = k_scale if c >= qk_cols else 1.0
        o_ref[:, c:c + half] = ((x1 * cos - x2 * sin) * sc).astype(BF16)
        o_ref[:, c + half:c + 2 * half] = ((x1 * sin + x2 * cos) * sc).astype(BF16)
    step = 2 * MXU_DIM
    for c in range(2 * qk_cols, n, step):
        acc = jnp.dot(hn, w_ref[:, c:c + step], preferred_element_type=F32)
        o_ref[:, c:c + step] = (_silu(acc) if c >= gate_col else acc).astype(BF16)


def _ret_proj(h, mods, gain, w, cos, sin, seq, qk_cols, gate_col, k_scale):
    t, d = h.shape
    n = w.shape[2]
    tm = ROW_TILE
    half = cos.shape[-1]
    return pl.pallas_call(
        functools.partial(_ret_proj_kernel, qk_cols=qk_cols, gate_col=gate_col, k_scale=k_scale),
        grid=(t // tm,),
        in_specs=[
            pl.BlockSpec((tm, d), lambda i: (i, 0)),
            _per_seq(mods, seq // tm),
            _resident((1, d)),
            _resident((d, n), layer=0),
            pl.BlockSpec((tm, half), lambda i: (i, 0)),
            pl.BlockSpec((tm, half), lambda i: (i, 0)),
        ],
        out_specs=pl.BlockSpec((tm, n), lambda i: (i, 0)),
        out_shape=jax.ShapeDtypeStruct((t, n), BF16),
        compiler_params=_params("parallel"),
        name="ret_proj",
    )(h, mods, gain.reshape(1, d), w, cos, sin)


def _retention_kernel(q_ref, k_ref, v_ref, g_ref, o_ref, state_ref, *, chunk):
    n_chunks = q_ref.shape[1] // chunk
    head = jnp.full((1, 1), pl.program_id(1), jnp.int32).astype(F32)
    log_gamma = jnp.log(1.0 - jnp.exp2(-5.0 - head))
    row = lax.broadcasted_iota(jnp.int32, (chunk, chunk), 0).astype(F32)
    col = lax.broadcasted_iota(jnp.int32, (chunk, chunk), 1).astype(F32)
    rel = row - col
    intra = jnp.where(rel >= 0, jnp.exp(jnp.maximum(rel, 0.0) * log_gamma), 0.0)
    idx = row[:, 0:1]
    q_decay = jnp.exp((idx + 1.0) * log_gamma)
    k_decay = jnp.exp((chunk - 1.0 - idx) * log_gamma)
    chunk_decay = jnp.exp(chunk * log_gamma)

    for n in range(n_chunks):
        rows = slice(n * chunk, (n + 1) * chunk)
        qc = q_ref[0, rows, :]
        kc = k_ref[0, rows, :]
        vc = v_ref[0, rows, :]
        o = _mxu(_mxu_nt(qc, kc) * intra, vc)
        if n > 0:
            o = o + _mxu(qc.astype(F32) * q_decay, state_ref[...])
        if n < n_chunks - 1:
            update = _mxu_tn(kc.astype(F32) * k_decay, vc)
            state_ref[...] = update if n == 0 else state_ref[...] * chunk_decay + update
        o_ref[0, rows, :] = (g_ref[0, rows, :].astype(F32) * _unit_rms(o)).astype(BF16)


def _retention(proj, heads, dk, dv):
    nb, s, _ = proj.shape
    kb0 = heads
    vb0 = 2 * heads * dk // dv
    gb0 = vb0 + heads
    return pl.pallas_call(
        functools.partial(_retention_kernel, chunk=RET_BLOCK),
        grid=(nb, heads),
        in_specs=[
            pl.BlockSpec((1, s, dk), lambda b, h: (b, 0, h)),
            pl.BlockSpec((1, s, dk), lambda b, h: (b, 0, kb0 + h)),
            pl.BlockSpec((1, s, dv), lambda b, h: (b, 0, vb0 + h)),
            pl.BlockSpec((1, s, dv), lambda b, h: (b, 0, gb0 + h)),
        ],
        out_specs=pl.BlockSpec((1, s, dv), lambda b, h: (b, 0, h)),
        out_shape=jax.ShapeDtypeStruct((nb, s, heads * dv), BF16),
        scratch_shapes=[pltpu.VMEM((dk, dv), F32)],
        compiler_params=_params("parallel", "parallel"),
        name="retention",
    )(proj, proj, proj, proj)


def _out_proj_kernel(*refs, gate_row, norm_rows, sub_rows):
    a_ref, w_ref, h_ref, mod_ref = refs[:4]
    n_norm = len(norm_rows)
    norm_in = refs[4:4 + 2 * n_norm]
    o_ref = refs[4 + 2 * n_norm]
    hn_refs = refs[5 + 2 * n_norm:]
    gate = mod_ref[0][gate_row:gate_row + 1, :]
    for r0 in range(0, a_ref.shape[0], sub_rows):
        rows = slice(r0, r0 + sub_rows)
        h = h_ref[rows, :] + gate * jnp.dot(a_ref[rows, :], w_ref[...], preferred_element_type=F32)
        o_ref[rows, :] = h
        if n_norm:
            xn = _unit_rms(h)
            for i, (shift_row, scale_row) in enumerate(norm_rows):
                gain = norm_in[2 * i][...]
                mod = norm_in[2 * i + 1][0]
                hn_refs[i][rows, :] = (xn * gain * (1.0 + mod[scale_row:scale_row + 1, :])
                                       + mod[shift_row:shift_row + 1, :]).astype(BF16)


def _out_proj(a, w, layer, h, mods, gate_row, seq, norms=()):
    t, k = a.shape
    d = w.shape[2]
    tm = OUT_ROW_TILE
    row_spec = pl.BlockSpec((tm, d), lambda i: (i, 0))
    in_specs = [pl.BlockSpec((tm, k), lambda i: (i, 0)), _resident((k, d), layer), row_spec,
                _per_seq(mods, seq // tm)]
    args = [a, w, h, mods]
    for gain, nmods, _, _ in norms:
        in_specs += [_resident((1, d)), _per_seq(nmods, seq // tm)]
        args += [gain.reshape(1, d), nmods]
    out = pl.pallas_call(
        functools.partial(_out_proj_kernel, gate_row=gate_row, sub_rows=MXU_DIM,
                          norm_rows=tuple((sh, sc) for _, _, sh, sc in norms)),
        grid=(t // tm,),
        in_specs=in_specs,
        out_specs=[row_spec] * (1 + len(norms)),
        out_shape=[jax.ShapeDtypeStruct((t, d), F32)]
        + [jax.ShapeDtypeStruct((t, d), BF16)] * len(norms),
        compiler_params=_params("parallel"),
        name="out_proj",
    )(*args)
    return out


def _ffn_up_kernel(hn_ref, w_ref, cw_ref, cb_ref, o_ref, carry_ref, buf_ref, *, per_seq, col_tile):
    tm = hn_ref.shape[0]
    f = o_ref.shape[1]
    halo = SUBLANES

    @pl.when(pl.program_id(0) % per_seq == 0)
    def _():
        carry_ref[...] = jnp.zeros_like(carry_ref)

    hn = hn_ref[...]

    def conv(c, scale):
        u = jnp.dot(hn, w_ref[:, c:c + col_tile], preferred_element_type=F32)
        buf_ref[0:halo, :] = carry_ref[:, c:c + col_tile]
        buf_ref[halo:halo + tm, :] = u
        carry_ref[:, c:c + col_tile] = u[tm - halo:, :]
        taps = cw_ref[:, c:c + col_tile] * scale
        y = cb_ref[:, c:c + col_tile] * scale + buf_ref[halo - 2:halo - 2 + tm, :] * taps[0:1]
        y = y + buf_ref[halo - 1:halo - 1 + tm, :] * taps[1:2]
        return y + u * taps[2:3]

    for c in range(0, f, col_tile):
        val = conv(c, 1.0)
        half_gate = conv(f + c, 0.5)
        silu_gate = half_gate + half_gate * jnp.tanh(half_gate)
        o_ref[:, c:c + col_tile] = (val * silu_gate).astype(BF16)


def _ffn_up(hn, w, conv_w, conv_b, layer, seq):
    t, d = hn.shape
    f2 = w.shape[2]
    f = f2 // 2
    tm = ROW_TILE
    col_tile = MXU_DIM
    return pl.pallas_call(
        functools.partial(_ffn_up_kernel, per_seq=seq // tm, col_tile=col_tile),
        grid=(t // tm,),
        in_specs=[
            pl.BlockSpec((tm, d), lambda i: (i, 0)),
            _resident((d, f2), layer),
            _resident((CONV_WIDTH, f2), layer),
            _resident((1, f2), layer),
        ],
        out_specs=pl.BlockSpec((tm, f), lambda i: (i, 0)),
        out_shape=jax.ShapeDtypeStruct((t, f), BF16),
        scratch_shapes=[pltpu.VMEM((SUBLANES, f2), F32),
                        pltpu.VMEM((SUBLANES + tm, col_tile), F32)],
        compiler_params=_params("arbitrary"),
        name="ffn_up",
    )(hn, w, conv_w, conv_b.reshape(-1, 1, f2))


def _qkv_proj_kernel(hq_ref, hkv_ref, wq_ref, wkv_ref, qg_ref, kg_ref, q_ref, k_ref, v_ref,
                     *, head_dim, q_scale):
    tm, d = hq_ref.shape
    hq = hq_ref[...]
    hkv = hkv_ref[...]
    ct = MXU_DIM
    cols = range(0, d, ct)
    r = lax.broadcasted_iota(jnp.int32, (ct, ct), 0) // head_dim
    c = lax.broadcasted_iota(jnp.int32, (ct, ct), 1) // head_dim
    pool = jnp.where(r == c, 1.0 / head_dim, 0.0).astype(BF16)
    raw = [jnp.dot(hq, wq_ref[:, c0:c0 + ct], preferred_element_type=F32) for c0 in cols]
    raw += [jnp.dot(hkv, wkv_ref[:, c0:c0 + ct], preferred_element_type=F32) for c0 in cols]
    squares = jnp.concatenate([(x * x).astype(BF16) for x in raw], axis=0)
    inv_rms = lax.rsqrt(jnp.dot(squares, pool, preferred_element_type=F32) + EPS)
    for i, c0 in enumerate(cols):
        j = len(cols) + i
        q_ref[:, c0:c0 + ct] = (raw[i] * inv_rms[i * tm:(i + 1) * tm]
                                * (qg_ref[:, c0:c0 + ct] * q_scale)).astype(BF16)
        k_ref[:, c0:c0 + ct] = (raw[j] * inv_rms[j * tm:(j + 1) * tm]
                                * kg_ref[:, c0:c0 + ct]).astype(BF16)
    for c0 in cols:
        v_ref[:, c0:c0 + ct] = jnp.dot(hkv, wkv_ref[:, d + c0:d + c0 + ct],
                                       preferred_element_type=F32).astype(BF16)


def _qkv_proj(hq, hkv, wq, wkv, q_gain, k_gain, head_dim):
    t, d = hq.shape
    tm = OUT_ROW_TILE
    out = jax.ShapeDtypeStruct((t, d), BF16)
    row_spec = pl.BlockSpec((tm, d), lambda i: (i, 0))
    return pl.pallas_call(
        functools.partial(_qkv_proj_kernel, head_dim=head_dim,
                          q_scale=head_dim ** -0.5 * math.log2(math.e)),
        grid=(t // tm,),
        in_specs=[row_spec, row_spec, _resident((d, d), layer=0), _resident((d, 2 * d), layer=0),
                  _resident((1, d)), _resident((1, d))],
        out_specs=[row_spec, row_spec, row_spec],
        out_shape=[out, out, out],
        compiler_params=_params("parallel"),
        name="qkv_proj",
    )(hq, hkv, wq, wkv,
      jnp.tile(q_gain, d // head_dim).reshape(1, d), jnp.tile(k_gain, d // head_dim).reshape(1, d))


def _softplus2(z):
    return jnp.maximum(z, 0.0) + jnp.log2(1.0 + jnp.exp2(-jnp.abs(z)))


def _neg_suffix_sum(x, neg_upper):
    return jnp.dot(x, neg_upper, preferred_element_type=F32)


def _attn_kernel(q_ref, k_ref, v_ref, o_ref, *, tk, head_dim):
    assert tk & (tk - 1) == 0 and tk % LANES == 0
    tq = 2 * tk
    n_heads = LANES // head_dim
    half_rows = n_heads * tk
    n_blocks = q_ref.shape[1] // tq
    lane = lax.broadcasted_iota(jnp.int32, (1, LANES), 1)
    head_lanes = [(lane >= h * head_dim) & (lane < (h + 1) * head_dim) for h in range(n_heads)]
    row = lax.broadcasted_iota(jnp.int32, (tk, tk), 0)
    col = lax.broadcasted_iota(jnp.int32, (tk, tk), 1)
    neg_upper = jnp.where(row > col, -1.0, 0.0)
    row = lax.broadcasted_iota(jnp.int32, (half_rows, tk), 0)
    col = lax.broadcasted_iota(jnp.int32, (half_rows, tk), 1)
    diag_mask = col < (row & (tk - 1))

    def scores(qs, k0):
        return _mxu_nt(qs, k_ref[0, pl.ds(k0, tk), :])

    def consume(z, k0, mask, acc, later):
        sp = _softplus2(z)
        if mask is not None:
            sp = jnp.where(mask, sp, 0.0)
        after = _neg_suffix_sum(sp, neg_upper)
        p = jnp.exp2(z - sp + after + jnp.concatenate([later] * (tk // LANES), axis=1))
        if mask is not None:
            p = jnp.where(mask, p, 0.0)
        acc = acc + jnp.dot(p, v_ref[0, pl.ds(k0, tk), :].astype(F32), preferred_element_type=F32)
        return acc, later + (after[:, 0:1] - sp[:, 0:1])

    def step(qs, k0, mask, acc, later):
        return consume(scores(qs, k0), k0, mask, acc, later)

    for qi in range(n_blocks):
        q0 = qi * tq
        q1 = q0 + tk
        q_lo = q_ref[0, q0:q0 + tk, :]
        q_hi = q_ref[0, q1:q1 + tk, :]
        q_hi = jnp.concatenate([jnp.where(m, q_hi, jnp.zeros_like(q_hi)) for m in head_lanes], axis=0)
        q_lo = jnp.concatenate([jnp.where(m, q_lo, jnp.zeros_like(q_lo)) for m in head_lanes], axis=0)
        q_all = jnp.concatenate([q_lo, q_hi], axis=0)
        zeros = jnp.zeros((half_rows, LANES), F32)
        acc_hi, later_hi = step(q_hi, q1, diag_mask, zeros, zeros)
        acc_hi, later_hi = step(q_hi, q0, None, acc_hi, later_hi)
        acc_lo, later_lo = step(q_lo, q0, diag_mask, zeros, zeros)
        acc = jnp.concatenate([acc_lo, acc_hi], axis=0)
        later = jnp.concatenate([later_lo, later_hi], axis=0)
        for kb in range(2 * qi - 1, -1, -1):
            acc, later = step(q_all, kb * tk, None, acc, later)
        for half, r0 in enumerate((q0, q1)):
            base = half * half_rows
            o = sum(jnp.where(m, acc[base + h * tk:base + (h + 1) * tk], 0.0)
                    for h, m in enumerate(head_lanes))
            o_ref[0, r0:r0 + tk, :] = o.astype(BF16)


def _attention(q, k, v, head_dim):
    nb, s, d = q.shape
    spec = pl.BlockSpec((1, s, LANES), lambda b, p: (b, 0, p))
    return pl.pallas_call(
        functools.partial(_attn_kernel, tk=ATTN_TILE, head_dim=head_dim),
        grid=(nb, d // LANES),
        in_specs=[spec, spec, spec],
        out_specs=spec,
        out_shape=jax.ShapeDtypeStruct((nb, s, d), BF16),
        compiler_params=_params("parallel", "parallel"),
        name="sb_attention",
    )(q, k, v)


def kernel(x, c, positions, ada_w, ada_b, norm_mix_g, norm_ffn_g, ret_w_in, ret_w_out, kv_ada_w, kv_ada_b, kv_norm_g, w_kv, k_norm_g, sb_w_q, q_norm_g, sb_w_out, ffn_w_in, ffn_conv_w, ffn_conv_b, ffn_w_out):
    nb, s, d = x.shape
    t = nb * s
    depth = ada_w.shape[0]
    n_mod = ada_w.shape[2] // d
    dk = d // RET_HEADS
    dv = 2 * dk
    head_dim = d // SB_HEADS
    shift_m, scale_m, gate_m, shift_f, scale_f, gate_f = range(6)

    mods = _ada(c, ada_w, ada_b).reshape(depth, nb, n_mod, d)
    kvmods = _ada(c, kv_ada_w[None], kv_ada_b[None]).reshape(nb, 2, d)
    cos, sin = _rope_tables(positions, dk // 2)
    cos = cos.reshape(t, dk // 2)
    sin = sin.reshape(t, dk // 2)

    h = x.reshape(t, d)

    ffn_w_in = ffn_w_in.astype(BF16)
    ffn_w_out = ffn_w_out.astype(BF16)

    proj = _ret_proj(h, mods[0], norm_mix_g[0], ret_w_in.astype(BF16), cos, sin,
                     s, RET_HEADS * dk, 2 * RET_HEADS * dk + RET_HEADS * dv, dk ** -0.5)
    og = _retention(proj.reshape(nb, s, -1), RET_HEADS, dk, dv)
    h, hf = _out_proj(og.reshape(t, -1), ret_w_out.astype(BF16), 0, h, mods[0], gate_m, s,
                      norms=[(norm_ffn_g[0], mods[0], shift_f, scale_f)])
    act = _ffn_up(hf, ffn_w_in, ffn_conv_w, ffn_conv_b, 0, s)
    h, hq, hkv = _out_proj(act, ffn_w_out, 0, h, mods[0], gate_f, s,
                           norms=[(norm_mix_g[1], mods[1], shift_m, scale_m),
                                  (kv_norm_g, kvmods, 0, 1)])

    q, k, v = _qkv_proj(hq, hkv, sb_w_q.astype(BF16), w_kv[None].astype(BF16),
                        q_norm_g[0], k_norm_g, head_dim)
    o = _attention(q.reshape(nb, s, d), k.reshape(nb, s, d), v.reshape(nb, s, d), head_dim)
    h, hf = _out_proj(o.reshape(t, d), sb_w_out.astype(BF16), 0, h, mods[1], gate_m, s,
                      norms=[(norm_ffn_g[1], mods[1], shift_f, scale_f)])
    act = _ffn_up(hf, ffn_w_in, ffn_conv_w, ffn_conv_b, 1, s)
    (h,) = _out_proj(act, ffn_w_out, 1, h, mods[1], gate_f, s)
    return h.reshape(nb, s, d)
```

```python
import functools
import math

import jax
import jax.numpy as jnp
from jax import lax
from jax.experimental import pallas as pl
from jax.experimental.pallas import tpu as pltpu

F32 = jnp.float32
BF16 = jnp.bfloat16

EPS = 1e-6
ROPE_BASE = 10000.0
RET_HEADS = 4
SB_HEADS = 16
CONV_WIDTH = 3

LANES = 128
SUBLANES = 8
MXU_DIM = 256
V7X_VMEM_BYTES = 64 * 1024 * 1024
VMEM_LIMIT_BYTES = (V7X_VMEM_BYTES * 7) // 8

ROW_TILE = 512
OUT_ROW_TILE = 1024
ATTN_TILE = 256
RET_BLOCK = MXU_DIM


def _params(*semantics):
    return pltpu.CompilerParams(dimension_semantics=semantics,
                                vmem_limit_bytes=VMEM_LIMIT_BYTES)


def _resident(shape, layer=None):
    zeros = (0,) * len(shape)
    if layer is None:
        return pl.BlockSpec(shape, lambda *_: zeros, pipeline_mode=pl.Buffered(1))
    return pl.BlockSpec((None,) + shape, lambda *_: (layer,) + zeros, pipeline_mode=pl.Buffered(1))


def _per_seq(mods, tiles_per_seq):
    return pl.BlockSpec((1,) + mods.shape[1:], lambda i: (i // tiles_per_seq, 0, 0))


def _silu(x):
    h = 0.5 * x
    return h + h * jnp.tanh(h)


def _unit_rms(x):
    return x * lax.rsqrt(jnp.mean(x * x, axis=-1, keepdims=True) + EPS)


def _mxu(a, b):
    return jnp.dot(a.astype(BF16), b.astype(BF16), preferred_element_type=F32)


def _mxu_nt(a, b):
    return lax.dot_general(a.astype(BF16), b.astype(BF16), (((1,), (1,)), ((), ())),
                           preferred_element_type=F32)


def _mxu_tn(a, b):
    return lax.dot_general(a.astype(BF16), b.astype(BF16), (((0,), (0,)), ((), ())),
                           preferred_element_type=F32)


def _ada_kernel(c_ref, w_ref, b_ref, o_ref):
    o_ref[...] = _mxu(_silu(c_ref[...]), w_ref[...]) + b_ref[...]


def _ada(c, w, b, col_tile=2048):
    nb, d = c.shape
    nl, _, n = w.shape
    return pl.pallas_call(
        _ada_kernel,
        grid=(nl, n // col_tile),
        in_specs=[
            pl.BlockSpec((nb, d), lambda l, j: (0, 0)),
            pl.BlockSpec((None, d, col_tile), lambda l, j: (l, 0, j)),
            pl.BlockSpec((None, 1, col_tile), lambda l, j: (l, 0, j)),
        ],
        out_specs=pl.BlockSpec((None, nb, col_tile), lambda l, j: (l, 0, j)),
        out_shape=jax.ShapeDtypeStruct((nl, nb, n), F32),
        compiler_params=_params("parallel", "parallel"),
        name="ada_mods",
    )(c, w, b.reshape(nl, 1, n))


def _rope_kernel(pos_ref, cos_ref, sin_ref, *, half):
    i = lax.broadcasted_iota(jnp.int32, (1, half), 1).astype(F32)
    inv = jnp.exp(i * (-math.log(ROPE_BASE) / half))
    pos = jnp.broadcast_to(pos_ref[0].astype(F32), (SUBLANES, pos_ref.shape[2]))
    ang = pos.T[:, 0:1] * inv
    cos_ref[0] = jnp.cos(ang)
    sin_ref[0] = jnp.sin(ang)


def _rope_tables(positions, half):
    nb, s = positions.shape
    out = jax.ShapeDtypeStruct((nb, s, half), F32)
    return pl.pallas_call(
        functools.partial(_rope_kernel, half=half),
        grid=(nb,),
        in_specs=[pl.BlockSpec((1, 1, s), lambda b: (b, 0, 0))],
        out_specs=[pl.BlockSpec((1, s, half), lambda b: (b, 0, 0))] * 2,
        out_shape=[out, out],
        compiler_params=_params("parallel"),
        name="rope_tables",
    )(positions.reshape(nb, 1, s))


def _ret_proj_kernel(h_ref, mod_ref, g_ref, w_ref, cos_ref, sin_ref, o_ref,
                     *, qk_cols, gate_col, k_scale):
    mod = mod_ref[0]
    hn = (_unit_rms(h_ref[...]) * g_ref[...] * (1.0 + mod[1:2, :]) + mod[0:1, :]).astype(BF16)
    cos = cos_ref[...]
    sin = sin_ref[...]
    half = cos.shape[-1]
    n = w_ref.shape[1]
    for c in range(0, 2 * qk_cols, 2 * half):
        acc = jnp.dot(hn, w_ref[:, c:c + 2 * half], preferred_element_type=F32)
        x1 = acc[:, :half]
        x2 = acc[:, half:]
        sc = k_scale if c >= qk_cols else 1.0
        o_ref[:, c:c + half] = ((x1 * cos - x2 * sin) * sc).astype(BF16)
        o_ref[:, c + half:c + 2 * half] = ((x1 * sin + x2 * cos) * sc).astype(BF16)
    step = 2 * MXU_DIM
    for c in range(2 * qk_cols, n, step):
        acc = jnp.dot(hn, w_ref[:, c:c + step], preferred_element_type=F32)
        o_ref[:, c:c + step] = (_silu(acc) if c >= gate_col else acc).astype(BF16)


def _ret_proj(h, mods, gain, w, cos, sin, seq, qk_cols, gate_col, k_scale):
    t, d = h.shape
    n = w.shape[2]
    tm = ROW_TILE
    half = cos.shape[-1]
    return pl.pallas_call(
        functools.partial(_ret_proj_kernel, qk_cols=qk_cols, gate_col=gate_col, k_scale=k_scale),
        grid=(t // tm,),
        in_specs=[
            pl.BlockSpec((tm, d), lambda i: (i, 0)),
            _per_seq(mods, seq // tm),
            _resident((1, d)),
            _resident((d, n), layer=0),
            pl.BlockSpec((tm, half), lambda i: (i, 0)),
            pl.BlockSpec((tm, half), lambda i: (i, 0)),
        ],
        out_specs=pl.BlockSpec((tm, n), lambda i: (i, 0)),
        out_shape=jax.ShapeDtypeStruct((t, n), BF16),
        compiler_params=_params("parallel"),
        name="ret_proj",
    )(h, mods, gain.reshape(1, d), w, cos, sin)


def _retention_kernel(q_ref, k_ref, v_ref, g_ref, o_ref, state_ref, *, chunk, dk, dv):
    n_chunks = q_ref.shape[1] // chunk
    n_heads = state_ref.shape[0]
    row = lax.broadcasted_iota(jnp.int32, (chunk, chunk), 0).astype(F32)
    col = lax.broadcasted_iota(jnp.int32, (chunk, chunk), 1).astype(F32)
    rel = row - col
    idx = row[:, 0:1]
    decays = []
    for hh in range(n_heads):
        head = jnp.full((1, 1), pl.program_id(1) * n_heads + hh, jnp.int32).astype(F32)
        log_gamma = jnp.log(1.0 - jnp.exp2(-5.0 - head))
        decays.append((
            jnp.where(rel >= 0, jnp.exp(jnp.maximum(rel, 0.0) * log_gamma), 0.0),
            jnp.exp((idx + 1.0) * log_gamma),
            jnp.exp((chunk - 1.0 - idx) * log_gamma),
            jnp.exp(chunk * log_gamma)))

    for n in range(n_chunks):
        rows = slice(n * chunk, (n + 1) * chunk)
        for hh, (intra, q_decay, k_decay, chunk_decay) in enumerate(decays):
            qc = q_ref[0, rows, hh * dk:(hh + 1) * dk]
            kc = k_ref[0, rows, hh * dk:(hh + 1) * dk]
            vc = v_ref[0, rows, hh * dv:(hh + 1) * dv]
            o = _mxu(_mxu_nt(qc, kc) * intra, vc)
            if n > 0:
                o = o + _mxu(qc.astype(F32) * q_decay, state_ref[hh])
            if n < n_chunks - 1:
                update = _mxu_tn(kc.astype(F32) * k_decay, vc)
                state_ref[hh] = update if n == 0 else state_ref[hh] * chunk_decay + update
            gate = g_ref[0, rows, hh * dv:(hh + 1) * dv].astype(F32)
            o_ref[0, rows, hh * dv:(hh + 1) * dv] = (gate * _unit_rms(o)).astype(BF16)


def _retention(proj, heads, dk, dv, heads_per_step=2):
    nb, s, _ = proj.shape
    steps = heads // heads_per_step
    wk = heads_per_step * dk
    wv = heads_per_step * dv
    kb0 = heads * dk // wk
    vb0 = 2 * heads * dk // wv
    gb0 = vb0 + steps
    return pl.pallas_call(
        functools.partial(_retention_kernel, chunk=RET_BLOCK, dk=dk, dv=dv),
        grid=(nb, steps),
        in_specs=[
            pl.BlockSpec((1, s, wk), lambda b, h: (b, 0, h)),
            pl.BlockSpec((1, s, wk), lambda b, h: (b, 0, kb0 + h)),
            pl.BlockSpec((1, s, wv), lambda b, h: (b, 0, vb0 + h)),
            pl.BlockSpec((1, s, wv), lambda b, h: (b, 0, gb0 + h)),
        ],
        out_specs=pl.BlockSpec((1, s, wv), lambda b, h: (b, 0, h)),
        out_shape=jax.ShapeDtypeStruct((nb, s, heads * dv), BF16),
        scratch_shapes=[pltpu.VMEM((heads_per_step, dk, dv), F32)],
        compiler_params=_params("parallel", "parallel"),
        name="retention",
    )(proj, proj, proj, proj)


def _out_proj_kernel(*refs, gate_row, norm_rows, sub_rows):
    a_ref, w_ref, h_ref, mod_ref = refs[:4]
    n_norm = len(norm_rows)
    norm_in = refs[4:4 + 2 * n_norm]
    o_ref = refs[4 + 2 * n_norm]
    hn_refs = refs[5 + 2 * n_norm:]
    gate = mod_ref[0][gate_row:gate_row + 1, :]
    for r0 in range(0, a_ref.shape[0], sub_rows):
        rows = slice(r0, r0 + sub_rows)
        h = h_ref[rows, :] + gate * jnp.dot(a_ref[rows, :], w_ref[...], preferred_element_type=F32)
        o_ref[rows, :] = h
        if n_norm:
            xn = _unit_rms(h)
            for i, (shift_row, scale_row) in enumerate(norm_rows):
                gain = norm_in[2 * i][...]
                mod = norm_in[2 * i + 1][0]
                hn_refs[i][rows, :] = (xn * gain * (1.0 + mod[scale_row:scale_row + 1, :])
                                       + mod[shift_row:shift_row + 1, :]).astype(BF16)


def _out_proj(a, w, layer, h, mods, gate_row, seq, norms=()):
    t, k = a.shape
    d = w.shape[2]
    tm = OUT_ROW_TILE
    row_spec = pl.BlockSpec((tm, d), lambda i: (i, 0))
    in_specs = [pl.BlockSpec((tm, k), lambda i: (i, 0)), _resident((k, d), layer), row_spec,
                _per_seq(mods, seq // tm)]
    args = [a, w, h, mods]
    for gain, nmods, _, _ in norms:
        in_specs += [_resident((1, d)), _per_seq(nmods, seq // tm)]
        args += [gain.reshape(1, d), nmods]
    out = pl.pallas_call(
        functools.partial(_out_proj_kernel, gate_row=gate_row, sub_rows=MXU_DIM,
                          norm_rows=tuple((sh, sc) for _, _, sh, sc in norms)),
        grid=(t // tm,),
        in_specs=in_specs,
        out_specs=[row_spec] * (1 + len(norms)),
        out_shape=[jax.ShapeDtypeStruct((t, d), F32)]
        + [jax.ShapeDtypeStruct((t, d), BF16)] * len(norms),
        compiler_params=_params("parallel"),
        name="out_proj",
    )(*args)
    return out


def _ffn_up_kernel(hn_ref, w_ref, cw_ref, cb_ref, o_ref, carry_ref, buf_ref, *, per_seq, col_tile):
    tm = hn_ref.shape[0]
    f = o_ref.shape[1]
    halo = SUBLANES

    @pl.when(pl.program_id(0) % per_seq == 0)
    def _():
        carry_ref[...] = jnp.zeros_like(carry_ref)

    hn = hn_ref[...]

    def conv(c, scale):
        u = jnp.dot(hn, w_ref[:, c:c + col_tile], preferred_element_type=F32)
        buf_ref[0:halo, :] = carry_ref[:, c:c + col_tile]
        buf_ref[halo:halo + tm, :] = u
        carry_ref[:, c:c + col_tile] = u[tm - halo:, :]
        taps = cw_ref[:, c:c + col_tile] * scale
        y = cb_ref[:, c:c + col_tile] * scale + buf_ref[halo - 2:halo - 2 + tm, :] * taps[0:1]
        y = y + buf_ref[halo - 1:halo - 1 + tm, :] * taps[1:2]
        return y + u * taps[2:3]

    for c in range(0, f, col_tile):
        val = conv(c, 1.0)
        half_gate = conv(f + c, 0.5)
        silu_gate = half_gate + half_gate * jnp.tanh(half_gate)
        o_ref[:, c:c + col_tile] = (val * silu_gate).astype(BF16)


def _ffn_up(hn, w, conv_w, conv_b, layer, seq):
    t, d = hn.shape
    f2 = w.shape[2]
    f = f2 // 2
    tm = ROW_TILE
    col_tile = MXU_DIM
    return pl.pallas_call(
        functools.partial(_ffn_up_kernel, per_seq=seq // tm, col_tile=col_tile),
        grid=(t // tm,),
        in_specs=[
            pl.BlockSpec((tm, d), lambda i: (i, 0)),
            _resident((d, f2), layer),
            _resident((CONV_WIDTH, f2), layer),
            _resident((1, f2), layer),
        ],
        out_specs=pl.BlockSpec((tm, f), lambda i: (i, 0)),
        out_shape=jax.ShapeDtypeStruct((t, f), BF16),
        scratch_shapes=[pltpu.VMEM((SUBLANES, f2), F32),
                        pltpu.VMEM((SUBLANES + tm, col_tile), F32)],
        compiler_params=_params("arbitrary"),
        name="ffn_up",
    )(hn, w, conv_w, conv_b.reshape(-1, 1, f2))


def _qkv_proj_kernel(hq_ref, hkv_ref, wq_ref, wkv_ref, qg_ref, kg_ref, q_ref, k_ref, v_ref,
                     *, head_dim, q_scale):
    tm, d = hq_ref.shape
    hq = hq_ref[...]
    hkv = hkv_ref[...]
    ct = MXU_DIM
    cols = range(0, d, ct)
    r = lax.broadcasted_iota(jnp.int32, (ct, ct), 0) // head_dim
    c = lax.broadcasted_iota(jnp.int32, (ct, ct), 1) // head_dim
    pool = jnp.where(r == c, 1.0 / head_dim, 0.0).astype(BF16)
    raw = [jnp.dot(hq, wq_ref[:, c0:c0 + ct], preferred_element_type=F32) for c0 in cols]
    raw += [jnp.dot(hkv, wkv_ref[:, c0:c0 + ct], preferred_element_type=F32) for c0 in cols]
    squares = jnp.concatenate([(x * x).astype(BF16) for x in raw], axis=0)
    inv_rms = lax.rsqrt(jnp.dot(squares, pool, preferred_element_type=F32) + EPS)
    for i, c0 in enumerate(cols):
        j = len(cols) + i
        q_ref[:, c0:c0 + ct] = (raw[i] * inv_rms[i * tm:(i + 1) * tm]
                                * (qg_ref[:, c0:c0 + ct] * q_scale)).astype(BF16)
        k_ref[:, c0:c0 + ct] = (raw[j] * inv_rms[j * tm:(j + 1) * tm]
                                * kg_ref[:, c0:c0 + ct]).astype(BF16)
    for c0 in cols:
        v_ref[:, c0:c0 + ct] = jnp.dot(hkv, wkv_ref[:, d + c0:d + c0 + ct],
                                       preferred_element_type=F32).astype(BF16)


def _qkv_proj(hq, hkv, wq, wkv, q_gain, k_gain, head_dim):
    t, d = hq.shape
    tm = OUT_ROW_TILE
    out = jax.ShapeDtypeStruct((t, d), BF16)
    row_spec = pl.BlockSpec((tm, d), lambda i: (i, 0))
    return pl.pallas_call(
        functools.partial(_qkv_proj_kernel, head_dim=head_dim,
                          q_scale=head_dim ** -0.5 * math.log2(math.e)),
        grid=(t // tm,),
        in_specs=[row_spec, row_spec, _resident((d, d), layer=0), _resident((d, 2 * d), layer=0),
                  _resident((1, d)), _resident((1, d))],
        out_specs=[row_spec, row_spec, row_spec],
        out_shape=[out, out, out],
        compiler_params=_params("parallel"),
        name="qkv_proj",
    )(hq, hkv, wq, wkv,
      jnp.tile(q_gain, d // head_dim).reshape(1, d), jnp.tile(k_gain, d // head_dim).reshape(1, d))


def _softplus2(z):
    return jnp.maximum(z, 0.0) + jnp.log2(1.0 + jnp.exp2(-jnp.abs(z)))


def _neg_suffix_sum(x, neg_upper):
    return jnp.dot(x, neg_upper, preferred_element_type=F32)


def _attn_kernel(q_ref, k_ref, v_ref, o_ref, *, tk, head_dim):
    assert tk & (tk - 1) == 0 and tk % LANES == 0
    tq = 2 * tk
    n_heads = LANES // head_dim
    half_rows = n_heads * tk
    n_blocks = q_ref.shape[1] // tq
    lane = lax.broadcasted_iota(jnp.int32, (1, LANES), 1)
    head_lanes = [(lane >= h * head_dim) & (lane < (h + 1) * head_dim) for h in range(n_heads)]
    row = lax.broadcasted_iota(jnp.int32, (tk, tk), 0)
    col = lax.broadcasted_iota(jnp.int32, (tk, tk), 1)
    neg_upper = jnp.where(row > col, -1.0, 0.0)
    row = lax.broadcasted_iota(jnp.int32, (half_rows, tk), 0)
    col = lax.broadcasted_iota(jnp.int32, (half_rows, tk), 1)
    diag_mask = col < (row & (tk - 1))

    def scores(qs, k0):
        return _mxu_nt(qs, k_ref[0, pl.ds(k0, tk), :])

    def consume(z, k0, mask, acc, later):
        sp = _softplus2(z)
        if mask is not None:
            sp = jnp.where(mask, sp, 0.0)
        after = _neg_suffix_sum(sp, neg_upper)
        p = jnp.exp2(z - sp + after + jnp.concatenate([later] * (tk // LANES), axis=1))
        if mask is not None:
            p = jnp.where(mask, p, 0.0)
        acc = acc + jnp.dot(p, v_ref[0, pl.ds(k0, tk), :].astype(F32), preferred_element_type=F32)
        return acc, later + (after[:, 0:1] - sp[:, 0:1])

    def step(qs, k0, mask, acc, later):
        return consume(scores(qs, k0), k0, mask, acc, later)

    for qi in range(n_blocks):
        q0 = qi * tq
        q1 = q0 + tk
        q_lo = q_ref[0, q0:q0 + tk, :]
        q_hi = q_ref[0, q1:q1 + tk, :]
        q_hi = jnp.concatenate([jnp.where(m, q_hi, jnp.zeros_like(q_hi)) for m in head_lanes], axis=0)
        q_lo = jnp.concatenate([jnp.where(m, q_lo, jnp.zeros_like(q_lo)) for m in head_lanes], axis=0)
        q_all = jnp.concatenate([q_lo, q_hi], axis=0)
        zeros = jnp.zeros((half_rows, LANES), F32)
        acc_hi, later_hi = step(q_hi, q1, diag_mask, zeros, zeros)
        acc_hi, later_hi = step(q_hi, q0, None, acc_hi, later_hi)
        acc_lo, later_lo = step(q_lo, q0, diag_mask, zeros, zeros)
        acc = jnp.concatenate([acc_lo, acc_hi], axis=0)
        later = jnp.concatenate([later_lo, later_hi], axis=0)
        for kb in range(2 * qi - 1, -1, -1):
            acc, later = step(q_all, kb * tk, None, acc, later)
        for half, r0 in enumerate((q0, q1)):
            base = half * half_rows
            o = sum(jnp.where(m, acc[base + h * tk:base + (h + 1) * tk], 0.0)
                    for h, m in enumerate(head_lanes))
            o_ref[0, r0:r0 + tk, :] = o.astype(BF16)


def _attention(q, k, v, head_dim):
    nb, s, d = q.shape
    spec = pl.BlockSpec((1, s, LANES), lambda b, p: (b, 0, p))
    return pl.pallas_call(
        functools.partial(_attn_kernel, tk=ATTN_TILE, head_dim=head_dim),
        grid=(nb, d // LANES),
        in_specs=[spec, spec, spec],
        out_specs=spec,
        out_shape=jax.ShapeDtypeStruct((nb, s, d), BF16),
        compiler_params=_params("parallel", "parallel"),
        name="sb_attention",
    )(q, k, v)


def kernel(x, c, positions, ada_w, ada_b, norm_mix_g, norm_ffn_g, ret_w_in, ret_w_out, kv_ada_w, kv_ada_b, kv_norm_g, w_kv, k_norm_g, sb_w_q, q_norm_g, sb_w_out, ffn_w_in, ffn_conv_w, ffn_conv_b, ffn_w_out):
    nb, s, d = x.shape
    t = nb * s
    depth = ada_w.shape[0]
    n_mod = ada_w.shape[2] // d
    dk = d // RET_HEADS
    dv = 2 * dk
    head_dim = d // SB_HEADS
    shift_m, scale_m, gate_m, shift_f, scale_f, gate_f = range(6)

    mods = _ada(c, ada_w, ada_b).reshape(depth, nb, n_mod, d)
    kvmods = _ada(c, kv_ada_w[None], kv_ada_b[None]).reshape(nb, 2, d)
    cos, sin = _rope_tables(positions, dk // 2)
    cos = cos.reshape(t, dk // 2)
    sin = sin.reshape(t, dk // 2)

    h = x.reshape(t, d)

    ffn_w_in = ffn_w_in.astype(BF16)
    ffn_w_out = ffn_w_out.astype(BF16)

    proj = _ret_proj(h, mods[0], norm_mix_g[0], ret_w_in.astype(BF16), cos, sin,
                     s, RET_HEADS * dk, 2 * RET_HEADS * dk + RET_HEADS * dv, dk ** -0.5)
    og = _retention(proj.reshape(nb, s, -1), RET_HEADS, dk, dv)
    h, hf = _out_proj(og.reshape(t, -1), ret_w_out.astype(BF16), 0, h, mods[0], gate_m, s,
                      norms=[(norm_ffn_g[0], mods[0], shift_f, scale_f)])
    act = _ffn_up(hf, ffn_w_in, ffn_conv_w, ffn_conv_b, 0, s)
    h, hq, hkv = _out_proj(act, ffn_w_out, 0, h, mods[0], gate_f, s,
                           norms=[(norm_mix_g[1], mods[1], shift_m, scale_m),
                                  (kv_norm_g, kvmods, 0, 1)])

    q, k, v = _qkv_proj(hq, hkv, sb_w_q.astype(BF16), w_kv[None].astype(BF16),
                        q_norm_g[0], k_norm_g, head_dim)
    o = _attention(q.reshape(nb, s, d), k.reshape(nb, s, d), v.reshape(nb, s, d), head_dim)
    h, hf = _out_proj(o.reshape(t, d), sb_w_out.astype(BF16), 0, h, mods[1], gate_m, s,
                      norms=[(norm_ffn_g[1], mods[1], shift_f, scale_f)])
    act = _ffn_up(hf, ffn_w_in, ffn_conv_w, ffn_conv_b, 1, s)
    (h,) = _out_proj(act, ffn_w_out, 1, h, mods[1], gate_f, s)
    return h.reshape(nb, s, d)
```

```python
import functools
import math

import jax
import jax.numpy as jnp
from jax import lax
from jax.experimental import pallas as pl
from jax.experimental.pallas import tpu as pltpu

F32 = jnp.float32
BF16 = jnp.bfloat16

EPS = 1e-6
ROPE_BASE = 10000.0
RET_HEADS = 4
SB_HEADS = 16
CONV_WIDTH = 3

LANES = 128
SUBLANES = 8
MXU_DIM = 256
V7X_VMEM_BYTES = 64 * 1024 * 1024
VMEM_LIMIT_BYTES = (V7X_VMEM_BYTES * 7) // 8

ROW_TILE = 512
OUT_ROW_TILE = 1024
ATTN_TILE = 256
RET_BLOCK = MXU_DIM


def _params(*semantics):
    return pltpu.CompilerParams(dimension_semantics=semantics,
                                vmem_limit_bytes=VMEM_LIMIT_BYTES)


def _resident(shape, layer=None):
    zeros = (0,) * len(shape)
    if layer is None:
        return pl.BlockSpec(shape, lambda *_: zeros, pipeline_mode=pl.Buffered(1))
    return pl.BlockSpec((None,) + shape, lambda *_: (layer,) + zeros, pipeline_mode=pl.Buffered(1))


def _per_seq(mods, tiles_per_seq):
    return pl.BlockSpec((1,) + mods.shape[1:], lambda i: (i // tiles_per_seq, 0, 0))


def _silu(x):
    h = 0.5 * x
    return h + h * jnp.tanh(h)


def _unit_rms(x):
    return x * lax.rsqrt(jnp.mean(x * x, axis=-1, keepdims=True) + EPS)


def _mxu(a, b):
    return jnp.dot(a.astype(BF16), b.astype(BF16), preferred_element_type=F32)


def _mxu_nt(a, b):
    return lax.dot_general(a.astype(BF16), b.astype(BF16), (((1,), (1,)), ((), ())),
                           preferred_element_type=F32)


def _mxu_tn(a, b):
    return lax.dot_general(a.astype(BF16), b.astype(BF16), (((0,), (0,)), ((), ())),
                           preferred_element_type=F32)


def _ada_kernel(c_ref, w_ref, b_ref, o_ref):
    o_ref[...] = _mxu(_silu(c_ref[...]), w_ref[...]) + b_ref[...]


def _ada(c, w, b, col_tile=2048):
    nb, d = c.shape
    nl, _, n = w.shape
    return pl.pallas_call(
        _ada_kernel,
        grid=(nl, n // col_tile),
        in_specs=[
            pl.BlockSpec((nb, d), lambda l, j: (0, 0)),
            pl.BlockSpec((None, d, col_tile), lambda l, j: (l, 0, j)),
            pl.BlockSpec((None, 1, col_tile), lambda l, j: (l, 0, j)),
        ],
        out_specs=pl.BlockSpec((None, nb, col_tile), lambda l, j: (l, 0, j)),
        out_shape=jax.ShapeDtypeStruct((nl, nb, n), F32),
        compiler_params=_params("parallel", "parallel"),
        name="ada_mods",
    )(c, w, b.reshape(nl, 1, n))


def _rope_kernel(pos_ref, cos_ref, sin_ref, *, half):
    i = lax.broadcasted_iota(jnp.int32, (1, half), 1).astype(F32)
    inv = jnp.exp(i * (-math.log(ROPE_BASE) / half))
    pos = jnp.broadcast_to(pos_ref[0].astype(F32), (SUBLANES, pos_ref.shape[2]))
    ang = pos.T[:, 0:1] * inv
    cos_ref[0] = jnp.cos(ang)
    sin_ref[0] = jnp.sin(ang)


def _rope_tables(positions, half):
    nb, s = positions.shape
    out = jax.ShapeDtypeStruct((nb, s, half), F32)
    return pl.pallas_call(
        functools.partial(_rope_kernel, half=half),
        grid=(nb,),
        in_specs=[pl.BlockSpec((1, 1, s), lambda b: (b, 0, 0))],
        out_specs=[pl.BlockSpec((1, s, half), lambda b: (b, 0, 0))] * 2,
        out_shape=[out, out],
        compiler_params=_params("parallel"),
        name="rope_tables",
    )(positions.reshape(nb, 1, s))


def _ret_proj_kernel(h_ref, mod_ref, g_ref, w_ref, cos_ref, sin_ref, o_ref,
                     *, qk_cols, gate_col, k_scale):
    mod = mod_ref[0]
    hn = (_unit_rms(h_ref[...]) * g_ref[...] * (1.0 + mod[1:2, :]) + mod[0:1, :]).astype(BF16)
    cos = cos_ref[...]
    sin = sin_ref[...]
    half = cos.shape[-1]
    n = w_ref.shape[1]
    for c in range(0, 2 * qk_cols, 2 * half):
        acc = jnp.dot(hn, w_ref[:, c:c + 2 * half], preferred_element_type=F32)
        x1 = acc[:, :half]
        x2 = acc[:, half:]
        sc = k_scale if c >= qk_cols else 1.0
        o_ref[:, c:c + half] = ((x1 * cos - x2 * sin) * sc).astype(BF16)
        o_ref[:, c + half:c + 2 * half] = ((x1 * sin + x2 * cos) * sc).astype(BF16)
    step = 2 * MXU_DIM
    for c in range(2 * qk_cols, n, step):
        acc = jnp.dot(hn, w_ref[:, c:c + step], preferred_element_type=F32)
        o_ref[:, c:c + step] = (_silu(acc) if c >= gate_col else acc).astype(BF16)


def _ret_proj(h, mods, gain, w, cos, sin, seq, qk_cols, gate_col, k_scale):
    t, d = h.shape
    n = w.shape[2]
    tm = ROW_TILE
    half = cos.shape[-1]
    return pl.pallas_call(
        functools.partial(_ret_proj_kernel, qk_cols=qk_cols, gate_col=gate_col, k_scale=k_scale),
        grid=(t // tm,),
        in_specs=[
            pl.BlockSpec((tm, d), lambda i: (i, 0)),
            _per_seq(mods, seq // tm),
            _resident((1, d)),
            _resident((d, n), layer=0),
            pl.BlockSpec((tm, half), lambda i: (i, 0)),
            pl.BlockSpec((tm, half), lambda i: (i, 0)),
        ],
        out_specs=pl.BlockSpec((tm, n), lambda i: (i, 0)),
        out_shape=jax.ShapeDtypeStruct((t, n), BF16),
        compiler_params=_params("parallel"),
        name="ret_proj",
    )(h, mods, gain.reshape(1, d), w, cos, sin)


def _retention_kernel(q_ref, k_ref, v_ref, g_ref, o_ref, state_ref, *, chunk, dk, dv):
    n_chunks = q_ref.shape[1] // chunk
    n_heads = state_ref.shape[0]
    row = lax.broadcasted_iota(jnp.int32, (chunk, chunk), 0).astype(F32)
    col = lax.broadcasted_iota(jnp.int32, (chunk, chunk), 1).astype(F32)
    rel = row - col
    idx = row[:, 0:1]
    decays = []
    for hh in range(n_heads):
        head = jnp.full((1, 1), pl.program_id(1) * n_heads + hh, jnp.int32).astype(F32)
        log_gamma = jnp.log(1.0 - jnp.exp2(-5.0 - head))
        decays.append((
            jnp.where(rel >= 0, jnp.exp(jnp.maximum(rel, 0.0) * log_gamma), 0.0),
            jnp.exp((idx + 1.0) * log_gamma),
            jnp.exp((chunk - 1.0 - idx) * log_gamma),
            jnp.exp(chunk * log_gamma)))

    for n in range(n_chunks):
        rows = slice(n * chunk, (n + 1) * chunk)
        for hh, (intra, q_decay, k_decay, chunk_decay) in enumerate(decays):
            qc = q_ref[0, rows, hh * dk:(hh + 1) * dk]
            kc = k_ref[0, rows, hh * dk:(hh + 1) * dk]
            vc = v_ref[0, rows, hh * dv:(hh + 1) * dv]
            o = _mxu(_mxu_nt(qc, kc) * intra, vc)
            if n > 0:
                o = o + _mxu(qc.astype(F32) * q_decay, state_ref[hh])
            if n < n_chunks - 1:
                update = _mxu_tn(kc.astype(F32) * k_decay, vc)
                state_ref[hh] = update if n == 0 else state_ref[hh] * chunk_decay + update
            gate = g_ref[0, rows, hh * dv:(hh + 1) * dv].astype(F32)
            o_ref[0, rows, hh * dv:(hh + 1) * dv] = (gate * _unit_rms(o)).astype(BF16)


def _retention(proj, heads, dk, dv, heads_per_step=2):
    nb, s, _ = proj.shape
    steps = heads // heads_per_step
    wk = heads_per_step * dk
    wv = heads_per_step * dv
    kb0 = heads * dk // wk
    vb0 = 2 * heads * dk // wv
    gb0 = vb0 + steps
    return pl.pallas_call(
        functools.partial(_retention_kernel, chunk=RET_BLOCK, dk=dk, dv=dv),
        grid=(nb, steps),
        in_specs=[
            pl.BlockSpec((1, s, wk), lambda b, h: (b, 0, h)),
            pl.BlockSpec((1, s, wk), lambda b, h: (b, 0, kb0 + h)),
            pl.BlockSpec((1, s, wv), lambda b, h: (b, 0, vb0 + h)),
            pl.BlockSpec((1, s, wv), lambda b, h: (b, 0, gb0 + h)),
        ],
        out_specs=pl.BlockSpec((1, s, wv), lambda b, h: (b, 0, h)),
        out_shape=jax.ShapeDtypeStruct((nb, s, heads * dv), BF16),
        scratch_shapes=[pltpu.VMEM((heads_per_step, dk, dv), F32)],
        compiler_params=_params("parallel", "parallel"),
        name="retention",
    )(proj, proj, proj, proj)


def _out_proj_kernel(*refs, gate_row, norm_rows, sub_rows):
    a_ref, w_ref, h_ref, mod_ref = refs[:4]
    n_norm = len(norm_rows)
    norm_in = refs[4:4 + 2 * n_norm]
    o_ref = refs[4 + 2 * n_norm]
    hn_refs = refs[5 + 2 * n_norm:]
    gate = mod_ref[0][gate_row:gate_row + 1, :]
    for r0 in range(0, a_ref.shape[0], sub_rows):
        rows = slice(r0, r0 + sub_rows)
        h = h_ref[rows, :] + gate * jnp.dot(a_ref[rows, :], w_ref[...], preferred_element_type=F32)
        o_ref[rows, :] = h
        if n_norm:
            xn = _unit_rms(h)
            for i, (shift_row, scale_row) in enumerate(norm_rows):
                gain = norm_in[2 * i][...]
                mod = norm_in[2 * i + 1][0]
                hn_refs[i][rows, :] = (xn * gain * (1.0 + mod[scale_row:scale_row + 1, :])
                                       + mod[shift_row:shift_row + 1, :]).astype(BF16)


def _out_proj(a, w, layer, h, mods, gate_row, seq, norms=()):
    t, k = a.shape
    d = w.shape[2]
    tm = OUT_ROW_TILE
    row_spec = pl.BlockSpec((tm, d), lambda i: (i, 0))
    in_specs = [pl.BlockSpec((tm, k), lambda i: (i, 0)), _resident((k, d), layer), row_spec,
                _per_seq(mods, seq // tm)]
    args = [a, w, h, mods]
    for gain, nmods, _, _ in norms:
        in_specs += [_resident((1, d)), _per_seq(nmods, seq // tm)]
        args += [gain.reshape(1, d), nmods]
    out = pl.pallas_call(
        functools.partial(_out_proj_kernel, gate_row=gate_row, sub_rows=MXU_DIM,
                          norm_rows=tuple((sh, sc) for _, _, sh, sc in norms)),
        grid=(t // tm,),
        in_specs=in_specs,
        out_specs=[row_spec] * (1 + len(norms)),
        out_shape=[jax.ShapeDtypeStruct((t, d), F32)]
        + [jax.ShapeDtypeStruct((t, d), BF16)] * len(norms),
        compiler_params=_params("parallel"),
        name="out_proj",
    )(*args)
    return out


def _ffn_up_kernel(hn_ref, w_ref, cw_ref, cb_ref, o_ref, carry_ref, buf_ref, *, per_seq, col_tile):
    tm = hn_ref.shape[0]
    f = o_ref.shape[1]
    halo = SUBLANES

    @pl.when(pl.program_id(0) % per_seq == 0)
    def _():
        carry_ref[...] = jnp.zeros_like(carry_ref)

    hn = hn_ref[...]

    def conv(c, scale):
        u = jnp.dot(hn, w_ref[:, c:c + col_tile], preferred_element_type=F32)
        buf_ref[0:halo, :] = carry_ref[:, c:c + col_tile]
        buf_ref[halo:halo + tm, :] = u
        carry_ref[:, c:c + col_tile] = u[tm - halo:, :]
        taps = cw_ref[:, c:c + col_tile] * scale
        y = cb_ref[:, c:c + col_tile] * scale + buf_ref[halo - 2:halo - 2 + tm, :] * taps[0:1]
        y = y + buf_ref[halo - 1:halo - 1 + tm, :] * taps[1:2]
        return y + u * taps[2:3]

    for c in range(0, f, col_tile):
        val = conv(c, 1.0)
        half_gate = conv(f + c, 0.5)
        silu_gate = half_gate + half_gate * jnp.tanh(half_gate)
        o_ref[:, c:c + col_tile] = (val * silu_gate).astype(BF16)


def _ffn_up(hn, w, conv_w, conv_b, layer, seq):
    t, d = hn.shape
    f2 = w.shape[2]
    f = f2 // 2
    tm = ROW_TILE
    col_tile = MXU_DIM
    return pl.pallas_call(
        functools.partial(_ffn_up_kernel, per_seq=seq // tm, col_tile=col_tile),
        grid=(t // tm,),
        in_specs=[
            pl.BlockSpec((tm, d), lambda i: (i, 0)),
            _resident((d, f2), layer),
            _resident((CONV_WIDTH, f2), layer),
            _resident((1, f2), layer),
        ],
        out_specs=pl.BlockSpec((tm, f), lambda i: (i, 0)),
        out_shape=jax.ShapeDtypeStruct((t, f), BF16),
        scratch_shapes=[pltpu.VMEM((SUBLANES, f2), F32),
                        pltpu.VMEM((SUBLANES + tm, col_tile), F32)],
        compiler_params=_params("arbitrary"),
        name="ffn_up",
    )(hn, w, conv_w, conv_b.reshape(-1, 1, f2))


def _qkv_proj_kernel(hq_ref, hkv_ref, wq_ref, wkv_ref, qg_ref, kg_ref, q_ref, k_ref, v_ref,
                     *, head_dim, q_scale):
    tm, d = hq_ref.shape
    hq = hq_ref[...]
    hkv = hkv_ref[...]
    ct = MXU_DIM
    cols = range(0, d, ct)
    r = lax.broadcasted_iota(jnp.int32, (ct, ct), 0) // head_dim
    c = lax.broadcasted_iota(jnp.int32, (ct, ct), 1) // head_dim
    pool = jnp.where(r == c, 1.0 / head_dim, 0.0).astype(BF16)
    raw = [jnp.dot(hq, wq_ref[:, c0:c0 + ct], preferred_element_type=F32) for c0 in cols]
    raw += [jnp.dot(hkv, wkv_ref[:, c0:c0 + ct], preferred_element_type=F32) for c0 in cols]
    squares = jnp.concatenate([(x * x).astype(BF16) for x in raw], axis=0)
    inv_rms = lax.rsqrt(jnp.dot(squares, pool, preferred_element_type=F32) + EPS)
    for i, c0 in enumerate(cols):
        j = len(cols) + i
        q_ref[:, c0:c0 + ct] = (raw[i] * inv_rms[i * tm:(i + 1) * tm]
                                * (qg_ref[:, c0:c0 + ct] * q_scale)).astype(BF16)
        k_ref[:, c0:c0 + ct] = (raw[j] * inv_rms[j * tm:(j + 1) * tm]
                                * kg_ref[:, c0:c0 + ct]).astype(BF16)
    for c0 in cols:
        v_ref[:, c0:c0 + ct] = jnp.dot(hkv, wkv_ref[:, d + c0:d + c0 + ct],
                                       preferred_element_type=F32).astype(BF16)


def _qkv_proj(hq, hkv, wq, wkv, q_gain, k_gain, head_dim):
    t, d = hq.shape
    tm = OUT_ROW_TILE
    out = jax.ShapeDtypeStruct((t, d), BF16)
    row_spec = pl.BlockSpec((tm, d), lambda i: (i, 0))
    return pl.pallas_call(
        functools.partial(_qkv_proj_kernel, head_dim=head_dim,
                          q_scale=head_dim ** -0.5 * math.log2(math.e)),
        grid=(t // tm,),
        in_specs=[row_spec, row_spec, _resident((d, d), layer=0), _resident((d, 2 * d), layer=0),
                  _resident((1, d)), _resident((1, d))],
        out_specs=[row_spec, row_spec, row_spec],
        out_shape=[out, out, out],
        compiler_params=_params("parallel"),
        name="qkv_proj",
    )(hq, hkv, wq, wkv,
      jnp.tile(q_gain, d // head_dim).reshape(1, d), jnp.tile(k_gain, d // head_dim).reshape(1, d))


def _softplus2(z):
    return jnp.maximum(z, 0.0) + jnp.log2(1.0 + jnp.exp2(-jnp.abs(z)))


def _neg_suffix_sum(x, neg_upper):
    return jnp.dot(x, neg_upper, preferred_element_type=F32)


def _attn_kernel(q_ref, k_ref, v_ref, o_ref, *, tk, head_dim):
    assert tk & (tk - 1) == 0 and tk % LANES == 0
    tq = 2 * tk
    n_heads = LANES // head_dim
    half_rows = n_heads * tk
    n_blocks = q_ref.shape[1] // tq
    lane = lax.broadcasted_iota(jnp.int32, (1, LANES), 1)
    head_lanes = [(lane >= h * head_dim) & (lane < (h + 1) * head_dim) for h in range(n_heads)]
    row = lax.broadcasted_iota(jnp.int32, (tk, tk), 0)
    col = lax.broadcasted_iota(jnp.int32, (tk, tk), 1)
    neg_upper = jnp.where(row > col, -1.0, 0.0)
    row = lax.broadcasted_iota(jnp.int32, (half_rows, tk), 0)
    col = lax.broadcasted_iota(jnp.int32, (half_rows, tk), 1)
    diag_mask = col < (row & (tk - 1))

    def step(qs, k0, lanes, mask, acc, later):
        z = _mxu_nt(qs, k_ref[0, pl.ds(k0, tk), lanes])
        sp = _softplus2(z)
        if mask is not None:
            sp = jnp.where(mask, sp, 0.0)
        after = _neg_suffix_sum(sp, neg_upper)
        p = jnp.exp2(z - sp + after + jnp.concatenate([later] * (tk // LANES), axis=1))
        if mask is not None:
            p = jnp.where(mask, p, 0.0)
        acc = acc + jnp.dot(p, v_ref[0, pl.ds(k0, tk), lanes].astype(F32),
                            preferred_element_type=F32)
        return acc, later + (after[:, 0:1] - sp[:, 0:1])

    for lg, qi in [(lg, qi) for qi in range(n_blocks) for lg in range(q_ref.shape[2] // LANES)]:
        lanes = slice(lg * LANES, (lg + 1) * LANES)
        q0 = qi * tq
        q1 = q0 + tk
        q_lo = q_ref[0, q0:q0 + tk, lanes]
        q_hi = q_ref[0, q1:q1 + tk, lanes]
        q_hi = jnp.concatenate([jnp.where(m, q_hi, jnp.zeros_like(q_hi)) for m in head_lanes], axis=0)
        q_lo = jnp.concatenate([jnp.where(m, q_lo, jnp.zeros_like(q_lo)) for m in head_lanes], axis=0)
        q_all = jnp.concatenate([q_lo, q_hi], axis=0)
        zeros = jnp.zeros((half_rows, LANES), F32)
        acc_hi, later_hi = step(q_hi, q1, lanes, diag_mask, zeros, zeros)
        acc_hi, later_hi = step(q_hi, q0, lanes, None, acc_hi, later_hi)
        acc_lo, later_lo = step(q_lo, q0, lanes, diag_mask, zeros, zeros)
        acc = jnp.concatenate([acc_lo, acc_hi], axis=0)
        later = jnp.concatenate([later_lo, later_hi], axis=0)
        for kb in range(2 * qi - 1, -1, -1):
            acc, later = step(q_all, kb * tk, lanes, None, acc, later)
        for half, r0 in enumerate((q0, q1)):
            base = half * half_rows
            o = sum(jnp.where(m, acc[base + h * tk:base + (h + 1) * tk], 0.0)
                    for h, m in enumerate(head_lanes))
            o_ref[0, r0:r0 + tk, lanes] = o.astype(BF16)


def _attention(q, k, v, head_dim):
    nb, s, d = q.shape
    width = 2 * LANES
    spec = pl.BlockSpec((1, s, width), lambda b, p: (b, 0, p))
    return pl.pallas_call(
        functools.partial(_attn_kernel, tk=ATTN_TILE, head_dim=head_dim),
        grid=(nb, d // width),
        in_specs=[spec, spec, spec],
        out_specs=spec,
        out_shape=jax.ShapeDtypeStruct((nb, s, d), BF16),
        compiler_params=_params("parallel", "parallel"),
        name="sb_attention",
    )(q, k, v)


def kernel(x, c, positions, ada_w, ada_b, norm_mix_g, norm_ffn_g, ret_w_in, ret_w_out, kv_ada_w, kv_ada_b, kv_norm_g, w_kv, k_norm_g, sb_w_q, q_norm_g, sb_w_out, ffn_w_in, ffn_conv_w, ffn_conv_b, ffn_w_out):
    nb, s, d = x.shape
    t = nb * s
    depth = ada_w.shape[0]
    n_mod = ada_w.shape[2] // d
    dk = d // RET_HEADS
    dv = 2 * dk
    head_dim = d // SB_HEADS
    shift_m, scale_m, gate_m, shift_f, scale_f, gate_f = range(6)

    mods = _ada(c, ada_w, ada_b).reshape(depth, nb, n_mod, d)
    kvmods = _ada(c, kv_ada_w[None], kv_ada_b[None]).reshape(nb, 2, d)
    cos, sin = _rope_tables(positions, dk // 2)
    cos = cos.reshape(t, dk // 2)
    sin = sin.reshape(t, dk // 2)

    h = x.reshape(t, d)

    ffn_w_in = ffn_w_in.astype(BF16)
    ffn_w_out = ffn_w_out.astype(BF16)

    proj = _ret_proj(h, mods[0], norm_mix_g[0], ret_w_in.astype(BF16), cos, sin,
                     s, RET_HEADS * dk, 2 * RET_HEADS * dk + RET_HEADS * dv, dk ** -0.5)
    og = _retention(proj.reshape(nb, s, -1), RET_HEADS, dk, dv)
    h, hf = _out_proj(og.reshape(t, -1), ret_w_out.astype(BF16), 0, h, mods[0], gate_m, s,
                      norms=[(norm_ffn_g[0], mods[0], shift_f, scale_f)])
    act = _ffn_up(hf, ffn_w_in, ffn_conv_w, ffn_conv_b, 0, s)
    h, hq, hkv = _out_proj(act, ffn_w_out, 0, h, mods[0], gate_f, s,
                           norms=[(norm_mix_g[1], mods[1], shift_m, scale_m),
                                  (kv_norm_g, kvmods, 0, 1)])

    q, k, v = _qkv_proj(hq, hkv, sb_w_q.astype(BF16), w_kv[None].astype(BF16),
                        q_norm_g[0], k_norm_g, head_dim)
    o = _attention(q.reshape(nb, s, d), k.reshape(nb, s, d), v.reshape(nb, s, d), head_dim)
    h, hf = _out_proj(o.reshape(t, d), sb_w_out.astype(BF16), 0, h, mods[1], gate_m, s,
                      norms=[(norm_ffn_g[1], mods[1], shift_f, scale_f)])
    act = _ffn_up(hf, ffn_w_in, ffn_conv_w, ffn_conv_b, 1, s)
    (h,) = _out_proj(act, ffn_w_out, 1, h, mods[1], gate_f, s)
    return h.reshape(nb, s, d)
```
